```python
import math
import jax, jax.numpy as jnp
from jax import lax
import numpy as np

D_MODEL = 4096
BATCH = 2
SEQ = 8192
DEPTH = 1

CONV_WIDTH = 2048
CONV_K = 3
N_HEADS = 16
N_KV_HEADS = 4
HEAD_DIM = 128
ATTN_WIDTH = N_HEADS * HEAD_DIM
KV_WIDTH = N_KV_HEADS * HEAD_DIM
WINDOW = 128
BLOCK = 128
MEM_LEN = 256
N_CROSS_HEADS = 4
CROSS_HEAD_DIM = 128
CROSS_WIDTH = N_CROSS_HEADS * CROSS_HEAD_DIM
N_EXPERTS = 16
EXPERT_FF = 2048
CAPACITY_FACTOR = 2
RMS_EPS = 1e-6
NEG_INF = -1e30

OFF_B = 0
OFF_C = OFF_B + CONV_WIDTH
OFF_U = OFF_C + CONV_WIDTH
OFF_Q = OFF_U + CONV_WIDTH
OFF_K = OFF_Q + ATTN_WIDTH
OFF_V = OFF_K + KV_WIDTH
OFF_GA = OFF_V + KV_WIDTH
OFF_GB = OFF_GA + D_MODEL
IN_COLS = OFF_GB + D_MODEL

kernel_name = "hybrid_gated_conv_swa_ec_moe_encoder"


def rmsnorm(x, g):
    xf = x.astype(jnp.float32)
    y = xf * lax.rsqrt(jnp.mean(xf * xf, axis=-1, keepdims=True) + RMS_EPS)
    return (y * g.astype(jnp.float32)).astype(x.dtype)


def alibi_slopes(n_heads):
    return jnp.power(2.0, -8.0 * (jnp.arange(n_heads, dtype=jnp.float32) + 1.0) / n_heads)


def short_conv_mixer(bg, cg, u, conv_w):
    z = cg * u
    zp = jnp.pad(z, ((0, 0), (1, 1), (0, 0)))
    w = conv_w.astype(z.dtype)
    conv = w[0] * zp[:, :-2] + w[1] * zp[:, 1:-1] + w[2] * zp[:, 2:]
    return bg * conv


def windowed_gqa(q, k, v, sinks):
    bsz, s_len, _, dh = q.shape
    nb = s_len // BLOCK
    g = N_HEADS // N_KV_HEADS
    qb = q.reshape(bsz, nb, BLOCK, N_KV_HEADS, g, dh)

    def band(t):
        tp = jnp.pad(t, ((0, 0), (BLOCK, BLOCK), (0, 0), (0, 0)))
        tp = tp.reshape(bsz, nb + 2, BLOCK, N_KV_HEADS, dh)
        return jnp.concatenate([tp[:, :-2], tp[:, 1:-1], tp[:, 2:]], axis=2)

    kb, vb = band(k), band(v)
    scale = 1.0 / math.sqrt(dh)
    s = jnp.einsum('bnqkgd,bnskd->bnkgqs', qb, kb, preferred_element_type=jnp.float32) * scale

    qpos = jnp.arange(s_len, dtype=jnp.int32).reshape(nb, BLOCK)
    kpos = (jnp.arange(nb, dtype=jnp.int32)[:, None] * BLOCK - BLOCK
            + jnp.arange(3 * BLOCK, dtype=jnp.int32)[None, :])
    dist = jnp.abs(qpos[:, :, None] - kpos[:, None, :])
    valid = (dist <= WINDOW) & (kpos >= 0)[:, None, :] & (kpos < s_len)[:, None, :]
    slopes = alibi_slopes(N_HEADS).reshape(N_KV_HEADS, g)
    bias = -slopes[None, :, :, None, None] * dist.astype(jnp.float32)[:, None, None, :, :]
    s = jnp.where(valid[:, None, None], s + bias[None], NEG_INF)

    sink = jnp.broadcast_to(sinks.astype(jnp.float32).reshape(1, 1, N_KV_HEADS, g, 1, 1),
                            s.shape[:-1] + (1,))
    p = jax.nn.softmax(jnp.concatenate([s, sink], axis=-1), axis=-1)[..., :-1]
    o = jnp.einsum('bnkgqs,bnskd->bnqkgd', p.astype(v.dtype), vb)
    return o.reshape(bsz, s_len, N_HEADS * dh)


def memory_cross_attention(h, mem_n, w_q, w_kv, w_o):
    bsz, s_len, _ = h.shape
    q = (h @ w_q).reshape(bsz, s_len, N_CROSS_HEADS, CROSS_HEAD_DIM)
    kv = mem_n @ w_kv
    k = kv[..., :CROSS_WIDTH].reshape(bsz, -1, N_CROSS_HEADS, CROSS_HEAD_DIM)
    v = kv[..., CROSS_WIDTH:].reshape(bsz, -1, N_CROSS_HEADS, CROSS_HEAD_DIM)
    s = jnp.einsum('bshd,bmhd->bhsm', q, k, preferred_element_type=jnp.float32) / math.sqrt(CROSS_HEAD_DIM)
    p = jax.nn.softmax(s, axis=-1)
    o = jnp.einsum('bhsm,bmhd->bshd', p.astype(v.dtype), v).reshape(bsz, s_len, CROSS_WIDTH)
    return o @ w_o


def expert_choice_moe(h, w_router, w_gate_e, w_up_e, w_down_e):
    bsz, t_len, d = h.shape
    cap = CAPACITY_FACTOR * t_len // N_EXPERTS
    logits = jnp.einsum('btd,de->bte', h, w_router, preferred_element_type=jnp.float32)
    aff = jax.nn.softmax(logits, axis=-1)
    gate, idx = lax.top_k(jnp.swapaxes(aff, 1, 2), cap)
    xg = jax.vmap(lambda hb, ib: hb[ib])(h, idx)
    a = jnp.einsum('becd,edf->becf', xg, w_gate_e)
    u = jnp.einsum('becd,edf->becf', xg, w_up_e)
    y = jnp.einsum('becf,efd->becd', jax.nn.silu(a) * u, w_down_e)
    y = y * gate[..., None].astype(y.dtype)
    return jax.vmap(lambda yb, ib: jax.ops.segment_sum(
        yb.reshape(-1, d), ib.reshape(-1), num_segments=t_len))(y, idx)


def setup_inputs(seed: int = 0) -> dict:
    key = jax.random.key(seed)
    ks = jax.random.split(key, 24)
    f32 = jnp.float32

    def w(k, shape, fan_in):
        return jax.random.normal(k, shape, f32) * (fan_in ** -0.5)

    def gain(k, n):
        return jnp.ones((n,), f32) + 0.02 * jax.random.normal(k, (n,), f32)

    return {
        "x": jax.random.normal(ks[0], (BATCH, SEQ, D_MODEL), f32),
        "mem": jax.random.normal(ks[1], (BATCH, MEM_LEN, D_MODEL), f32),
        "g_mix": gain(ks[2], D_MODEL),
        "w_in": w(ks[3], (D_MODEL, IN_COLS), D_MODEL),
        "conv_w": w(ks[4], (CONV_K, CONV_WIDTH), CONV_K),
        "attn_sinks": 0.5 * jax.random.normal(ks[5], (N_HEADS,), f32),
        "w_conv_out": w(ks[6], (CONV_WIDTH, D_MODEL), CONV_WIDTH),
        "w_attn_out": w(ks[7], (ATTN_WIDTH, D_MODEL), ATTN_WIDTH),
        "w_out": w(ks[8], (D_MODEL, D_MODEL), D_MODEL),
        "g_cross": gain(ks[9], D_MODEL),
        "g_mem": gain(ks[10], D_MODEL),
        "w_q_cross": w(ks[11], (D_MODEL, CROSS_WIDTH), D_MODEL),
        "w_kv_cross": w(ks[12], (D_MODEL, 2 * CROSS_WIDTH), D_MODEL),
        "w_o_cross": w(ks[13], (CROSS_WIDTH, D_MODEL), CROSS_WIDTH),
        "g_moe": gain(ks[14], D_MODEL),
        "w_router": w(ks[15], (D_MODEL, N_EXPERTS), D_MODEL),
        "w_gate_e": w(ks[16], (N_EXPERTS, D_MODEL, EXPERT_FF), D_MODEL),
        "w_up_e": w(ks[17], (N_EXPERTS, D_MODEL, EXPERT_FF), D_MODEL),
        "w_down_e": w(ks[18], (N_EXPERTS, EXPERT_FF, D_MODEL), EXPERT_FF),
        "g_final": gain(ks[19], D_MODEL),
    }


def reference(x, mem, g_mix, w_in, conv_w, attn_sinks, w_conv_out, w_attn_out, w_out,
              g_cross, g_mem, w_q_cross, w_kv_cross, w_o_cross,
              g_moe, w_router, w_gate_e, w_up_e, w_down_e, g_final):
    bsz, s_len, _ = x.shape
    mem_n = rmsnorm(mem, g_mem)
    for _layer in range(DEPTH):
        h = rmsnorm(x, g_mix)
        p = h @ w_in
        bg = p[..., OFF_B:OFF_C]
        cg = p[..., OFF_C:OFF_U]
        u = p[..., OFF_U:OFF_Q]
        q = p[..., OFF_Q:OFF_K].reshape(bsz, s_len, N_HEADS, HEAD_DIM)
        k = p[..., OFF_K:OFF_V].reshape(bsz, s_len, N_KV_HEADS, HEAD_DIM)
        v = p[..., OFF_V:OFF_GA].reshape(bsz, s_len, N_KV_HEADS, HEAD_DIM)
        gate_a = jax.nn.sigmoid(p[..., OFF_GA:OFF_GB])
        gate_b = jax.nn.sigmoid(p[..., OFF_GB:IN_COLS])

        y_a = short_conv_mixer(bg, cg, u, conv_w) @ w_conv_out
        y_b = windowed_gqa(q, k, v, attn_sinks) @ w_attn_out
        x = x + (gate_a * y_a + gate_b * y_b) @ w_out

        x = x + memory_cross_attention(rmsnorm(x, g_cross), mem_n, w_q_cross, w_kv_cross, w_o_cross)

        x = x + expert_choice_moe(rmsnorm(x, g_moe), w_router, w_gate_e, w_up_e, w_down_e)
    return rmsnorm(x, g_final)
```

```python
import functools
import math

import jax
import jax.numpy as jnp
from jax import lax
from jax.experimental import pallas as pl
from jax.experimental.pallas import tpu as pltpu

RMS_EPS = 1e-6
NEG_INF = -1e30
WINDOW = 128
BLOCK = 128
N_CROSS_HEADS = 4
CAPACITY_FACTOR = 2
BF16_SUBLANES = 16
COMBINE_WIN = BLOCK + BF16_SUBLANES
VMEM_LIMIT_BYTES = 56 * 1024 * 1024

F32 = jnp.float32
BF16 = jnp.bfloat16


def _cparams(semantics, vmem=VMEM_LIMIT_BYTES):
    return pltpu.CompilerParams(dimension_semantics=semantics, vmem_limit_bytes=vmem)


def _pick(prefs, *sizes):
    for t in prefs:
        if all(s % t == 0 for s in sizes):
            return t
    raise ValueError(f"no tile in {prefs} divides {sizes}")


def _rms(x, g):
    return x * lax.rsqrt(jnp.mean(x * x, axis=-1, keepdims=True) + RMS_EPS) * g


def _sigmoid(x):
    return 1.0 / (1.0 + jnp.exp(-x))


def _rmsnorm_kernel(x_ref, g_ref, o_ref):
    o_ref[...] = _rms(x_ref[...].astype(F32), g_ref[...]).astype(o_ref.dtype)


def _rmsnorm(x2d, g, out_dtype):
    m, d = x2d.shape
    tm = _pick((256, 128, 64, 8), m)
    return pl.pallas_call(
        _rmsnorm_kernel,
        out_shape=jax.ShapeDtypeStruct((m, d), out_dtype),
        grid=(m // tm,),
        in_specs=[pl.BlockSpec((tm, d), lambda i: (i, 0)), pl.BlockSpec((1, d), lambda i: (0, 0))],
        out_specs=pl.BlockSpec((tm, d), lambda i: (i, 0)),
        compiler_params=_cparams(("parallel",)),
        name="rmsnorm",
    )(x2d, g.reshape(1, d).astype(F32))


def _inproj_kernel(h_ref, w_ref, o_ref, *, gate_tile0):
    acc = jnp.dot(h_ref[...], w_ref[...], preferred_element_type=F32)
    j = pl.program_id(1)

    @pl.when(j < gate_tile0)
    def _():
        o_ref[...] = acc.astype(o_ref.dtype)

    @pl.when(j >= gate_tile0)
    def _():
        o_ref[...] = _sigmoid(acc).astype(o_ref.dtype)


def _inproj(h, w, off_gate):
    m, d = h.shape
    n = w.shape[1]
    tm = _pick((1024, 512, 256, 128), m)
    tn = _pick((1024, 512, 256, 128), n, off_gate)
    return pl.pallas_call(
        functools.partial(_inproj_kernel, gate_tile0=off_gate // tn),
        out_shape=jax.ShapeDtypeStruct((m, n), BF16),
        grid=(m // tm, n // tn),
        in_specs=[pl.BlockSpec((tm, d), lambda i, j: (i, 0)), pl.BlockSpec((d, tn), lambda i, j: (0, j))],
        out_specs=pl.BlockSpec((tm, tn), lambda i, j: (i, j)),
        compiler_params=_cparams(("parallel", "arbitrary")),
        name="inproj",
    )(h, w)


def _conv_kernel(b_ref, c_ref, u_ref, cp_ref, up_ref, cn_ref, un_ref, w_ref, o_ref, *, seq_blocks):
    i = pl.program_id(0)
    tm = c_ref.shape[0]
    z = c_ref[...].astype(F32) * u_ref[...].astype(F32)
    last_row = BF16_SUBLANES - 1
    z_prev = cp_ref[last_row:last_row + 1, :].astype(F32) * up_ref[last_row:last_row + 1, :].astype(F32)
    z_next = cn_ref[0:1, :].astype(F32) * un_ref[0:1, :].astype(F32)
    pos = i % seq_blocks
    z_prev = jnp.where(pos == 0, 0.0, z_prev)
    z_next = jnp.where(pos == seq_blocks - 1, 0.0, z_next)
    row = lax.broadcasted_iota(jnp.int32, z.shape, 0)
    z_up = jnp.where(row == 0, z_prev, pltpu.roll(z, 1, axis=0))
    z_dn = jnp.where(row == tm - 1, z_next, pltpu.roll(z, tm - 1, axis=0))
    w = w_ref[...]
    conv = w[0:1, :] * z_up + w[1:2, :] * z + w[2:3, :] * z_dn
    o_ref[...] = (b_ref[...].astype(F32) * conv).astype(o_ref.dtype)


def _conv_mixer(p, conv_w, seq, off_b, off_c, off_u):
    m = p.shape[0]
    cw = conv_w.shape[1]
    tm = _pick((512, 256, 128), seq)
    tc = _pick((512, 256, 128), cw, off_b, off_c, off_u)
    halo = BF16_SUBLANES
    rb = tm // halo
    n_halo = m // halo

    def main(off):
        return pl.BlockSpec((tm, tc), lambda i, j: (i, off // tc + j))

    def prev(off):
        return pl.BlockSpec((halo, tc), lambda i, j: (jnp.maximum(i * rb - 1, 0), off // tc + j))

    def nxt(off):
        return pl.BlockSpec((halo, tc), lambda i, j: (jnp.minimum((i + 1) * rb, n_halo - 1), off // tc + j))

    return pl.pallas_call(
        functools.partial(_conv_kernel, seq_blocks=seq // tm),
        out_shape=jax.ShapeDtypeStruct((m, cw), BF16),
        grid=(m // tm, cw // tc),
        in_specs=[main(off_b), main(off_c), main(off_u), prev(off_c), prev(off_u), nxt(off_c), nxt(off_u),
                  pl.BlockSpec((conv_w.shape[0], tc), lambda i, j: (0, j))],
        out_specs=pl.BlockSpec((tm, tc), lambda i, j: (i, j)),
        compiler_params=_cparams(("parallel", "parallel")),
        name="conv_mixer",
    )(p, p, p, p, p, p, p, conv_w.astype(F32))


def _swa_kernel(slopes_ref, sinks_ref, q_ref, kp_ref, kc_ref, kn_ref, vp_ref, vc_ref, vn_ref, o_ref,
                *, n_kv, group, hd, nb):
    n = pl.program_id(1)
    has_prev = n > 0
    has_next = n < nb - 1
    shape = (BLOCK, 3 * BLOCK)
    krel = lax.broadcasted_iota(jnp.int32, shape, 1) - BLOCK
    dist = jnp.abs(lax.broadcasted_iota(jnp.int32, shape, 0) - krel)
    valid = (dist <= WINDOW) & ((krel >= 0) | has_prev) & ((krel < BLOCK) | has_next)
    distf = dist.astype(F32)
    scale = 1.0 / math.sqrt(hd)
    for kh in range(n_kv):
        cols = slice(kh * hd, (kh + 1) * hd)
        kband = jnp.concatenate([kp_ref[:, cols], kc_ref[:, cols], kn_ref[:, cols]], axis=0)
        vband = jnp.concatenate([vp_ref[:, cols], vc_ref[:, cols], vn_ref[:, cols]], axis=0)
        for gi in range(group):
            h = kh * group + gi
            hcols = slice(h * hd, (h + 1) * hd)
            s = lax.dot_general(q_ref[:, hcols], kband, (((1,), (1,)), ((), ())),
                                preferred_element_type=F32) * scale
            s = jnp.where(valid, s - slopes_ref[h] * distf, NEG_INF)
            sink = sinks_ref[h]
            mx = jnp.maximum(jnp.max(s, axis=-1, keepdims=True), sink)
            pr = jnp.exp(s - mx)
            denom = jnp.sum(pr, axis=-1, keepdims=True) + jnp.exp(sink - mx)
            o = jnp.dot(pr.astype(BF16), vband, preferred_element_type=F32) / denom
            o_ref[:, hcols] = o.astype(o_ref.dtype)


def _windowed_attention(p, slopes, sinks, bsz, seq, off_q, off_k, off_v, aw, kvw, n_heads):
    m = p.shape[0]
    hd = aw // n_heads
    n_kv = kvw // hd
    nb = seq // BLOCK
    assert off_q % aw == 0 and off_k % kvw == 0 and off_v % kvw == 0 and seq % BLOCK == 0

    def kv_spec(off, shift):
        def imap(b, n, *_):
            return (b * nb + jnp.clip(n + shift, 0, nb - 1), off // kvw)
        return pl.BlockSpec((BLOCK, kvw), imap)

    grid_spec = pltpu.PrefetchScalarGridSpec(
        num_scalar_prefetch=2,
        grid=(bsz, nb),
        in_specs=[pl.BlockSpec((BLOCK, aw), lambda b, n, *_: (b * nb + n, off_q // aw)),
                  kv_spec(off_k, -1), kv_spec(off_k, 0), kv_spec(off_k, 1),
                  kv_spec(off_v, -1), kv_spec(off_v, 0), kv_spec(off_v, 1)],
        out_specs=pl.BlockSpec((BLOCK, aw), lambda b, n, *_: (b * nb + n, 0)),
    )
    return pl.pallas_call(
        functools.partial(_swa_kernel, n_kv=n_kv, group=n_heads // n_kv, hd=hd, nb=nb),
        out_shape=jax.ShapeDtypeStruct((m, aw), BF16),
        grid_spec=grid_spec,
        compiler_params=_cparams(("parallel", "parallel")),
        name="windowed_gqa",
    )(slopes, sinks, p, p, p, p, p, p, p)


def _merge_kernel(za_ref, ob_ref, wc_ref, wa_ref, ga_ref, gb_ref, o_ref):
    ya = jnp.dot(za_ref[...], wc_ref[...], preferred_element_type=F32)
    yb = jnp.dot(ob_ref[...], wa_ref[...], preferred_element_type=F32)
    o_ref[...] = (ga_ref[...].astype(F32) * ya + gb_ref[...].astype(F32) * yb).astype(o_ref.dtype)


def _merge(za, ob, wc, wa, p, off_ga, off_gb):
    m, ka = za.shape
    kb = ob.shape[1]
    d = wc.shape[1]
    tm = _pick((1024, 512, 256, 128), m)
    tn = _pick((512, 256, 128), d, off_ga, off_gb)
    return pl.pallas_call(
        _merge_kernel,
        out_shape=jax.ShapeDtypeStruct((m, d), BF16),
        grid=(m // tm, d // tn),
        in_specs=[pl.BlockSpec((tm, ka), lambda i, j: (i, 0)), pl.BlockSpec((tm, kb), lambda i, j: (i, 0)),
                  pl.BlockSpec((ka, tn), lambda i, j: (0, j)), pl.BlockSpec((kb, tn), lambda i, j: (0, j)),
                  pl.BlockSpec((tm, tn), lambda i, j: (i, off_ga // tn + j)),
                  pl.BlockSpec((tm, tn), lambda i, j: (i, off_gb // tn + j))],
        out_specs=pl.BlockSpec((tm, tn), lambda i, j: (i, j)),
        compiler_params=_cparams(("parallel", "arbitrary")),
        name="gated_merge",
    )(za, ob, wc, wa, p, p)


def _resid_matmul_kernel(a_ref, w_ref, x_ref, o_ref):
    o_ref[...] = x_ref[...] + jnp.dot(a_ref[...], w_ref[...], preferred_element_type=F32)


def _resid_matmul(a, w, x):
    m, k = a.shape
    n = w.shape[1]
    tm = _pick((1024, 512, 256, 128), m)
    tn = _pick((512, 256, 128), n)
    return pl.pallas_call(
        _resid_matmul_kernel,
        out_shape=jax.ShapeDtypeStruct((m, n), F32),
        grid=(m // tm, n // tn),
        in_specs=[pl.BlockSpec((tm, k), lambda i, j: (i, 0)), pl.BlockSpec((k, tn), lambda i, j: (0, j)),
                  pl.BlockSpec((tm, tn), lambda i, j: (i, j))],
        out_specs=pl.BlockSpec((tm, tn), lambda i, j: (i, j)),
        compiler_params=_cparams(("parallel", "arbitrary")),
        name="out_proj_residual",
    )(a, w, x)


def _mem_kv_kernel(mem_ref, g_ref, w_ref, o_ref):
    mn = _rms(mem_ref[...].astype(F32), g_ref[...]).astype(BF16)
    o_ref[...] = jnp.dot(mn, w_ref[...], preferred_element_type=F32).astype(o_ref.dtype)


def _mem_kv(mem2d, g, w):
    m, d = mem2d.shape
    n = w.shape[1]
    tn = _pick((512, 256, 128), n)
    return pl.pallas_call(
        _mem_kv_kernel,
        out_shape=jax.ShapeDtypeStruct((m, n), BF16),
        grid=(n // tn,),
        in_specs=[pl.BlockSpec((m, d), lambda j: (0, 0)), pl.BlockSpec((1, d), lambda j: (0, 0)),
                  pl.BlockSpec((d, tn), lambda j: (0, j))],
        out_specs=pl.BlockSpec((m, tn), lambda j: (0, j)),
        compiler_params=_cparams(("parallel",)),
        name="mem_kv",
    )(mem2d, g.reshape(1, d).astype(F32), w)


def _pack_halves(hb):
    half = hb.shape[1] // 2
    lo = lax.bitcast_convert_type(hb[:, :half].astype(F32), jnp.uint32)
    hi = lax.bitcast_convert_type(hb[:, half:].astype(F32), jnp.uint32)
    return (lo >> 16) | (hi & jnp.uint32(0xFFFF0000))


def _unpack_halves(w):
    lo = lax.bitcast_convert_type(w << 16, F32).astype(BF16)
    hi = lax.bitcast_convert_type(w & jnp.uint32(0xFFFF0000), F32).astype(BF16)
    return lo, hi


def _cross_kernel(x_ref, gc_ref, wq_ref, kv_ref, wo_ref, gm_ref, wrh_ref, wrl_ref,
                  x2_ref, hm_ref, lg_ref, *, cw):
    x = x_ref[...]
    hc = _rms(x, gc_ref[...]).astype(BF16)
    q = jnp.dot(hc, wq_ref[...], preferred_element_type=F32).astype(BF16)
    hd = cw // N_CROSS_HEADS
    inv = 1.0 / math.sqrt(hd)
    outs = []
    for h in range(N_CROSS_HEADS):
        k = kv_ref[0, :, h * hd:(h + 1) * hd]
        v = kv_ref[0, :, cw + h * hd:cw + (h + 1) * hd]
        s = lax.dot_general(q[:, h * hd:(h + 1) * hd], k, (((1,), (1,)), ((), ())),
                            preferred_element_type=F32) * inv
        pr = jnp.exp(s - jnp.max(s, axis=-1, keepdims=True))
        o = jnp.dot(pr.astype(BF16), v, preferred_element_type=F32) / jnp.sum(pr, axis=-1, keepdims=True)
        outs.append(o.astype(BF16))
    o = jnp.concatenate(outs, axis=1)
    x2 = x + jnp.dot(o, wo_ref[...], preferred_element_type=F32)
    x2_ref[...] = x2
    hm = _rms(x2, gm_ref[...])
    hi = hm.astype(BF16)
    lo = (hm - hi.astype(F32)).astype(BF16)
    hm_ref[...] = _pack_halves(hi)
    lg_ref[...] = (jnp.dot(hi, wrh_ref[...], preferred_element_type=F32)
                   + jnp.dot(lo, wrh_ref[...], preferred_element_type=F32)
                   + jnp.dot(hi, wrl_ref[...], preferred_element_type=F32))


def _cross_attention(x1, g_cross, wq, kv, wo, g_moe, wr_hi, wr_lo, seq):
    m, d = x1.shape
    cw = wq.shape[1]
    ne = wr_hi.shape[1]
    mem_len = kv.shape[1]
    tm = _pick((256, 128), seq)
    per_seq = seq // tm
    const = lambda i: (0, 0)
    return pl.pallas_call(
        functools.partial(_cross_kernel, cw=cw),
        out_shape=(jax.ShapeDtypeStruct((m, d), F32), jax.ShapeDtypeStruct((m, d // 2), jnp.uint32),
                   jax.ShapeDtypeStruct((m, ne), F32)),
        grid=(m // tm,),
        in_specs=[pl.BlockSpec((tm, d), lambda i: (i, 0)), pl.BlockSpec((1, d), const),
                  pl.BlockSpec((d, cw), const),
                  pl.BlockSpec((1, mem_len, 2 * cw), lambda i: (i // per_seq, 0, 0)),
                  pl.BlockSpec((cw, d), const), pl.BlockSpec((1, d), const),
                  pl.BlockSpec((d, ne), const), pl.BlockSpec((d, ne), const)],
        out_specs=(pl.BlockSpec((tm, d), lambda i: (i, 0)), pl.BlockSpec((tm, d // 2), lambda i: (i, 0)),
                   pl.BlockSpec((tm, ne), lambda i: (i, 0))),
        compiler_params=_cparams(("parallel",)),
        name="cross_attention",
    )(x1, g_cross.reshape(1, d).astype(F32), wq, kv, wo, g_moe.reshape(1, d).astype(F32), wr_hi, wr_lo)


def _route_kernel(lg_ref, idx_ref, pos_ref, aff_ref, ends_ref, gt_ref, eq_ref, cgt_ref, ceq_ref, csel_ref,
                  *, cap, chunk):
    logits = lg_ref[0]
    t_len, ne = logits.shape
    ex = jnp.exp(logits - jnp.max(logits, axis=-1, keepdims=True))
    aff = ex / jnp.sum(ex, axis=-1, keepdims=True)
    aff_ref[0] = aff
    keys = lax.bitcast_convert_type(aff, jnp.int32)

    def search(it, thr):
        cand = thr | jnp.left_shift(jnp.int32(1), 30 - it)
        cnt = jnp.sum((keys >= cand).astype(jnp.int32), axis=0, keepdims=True)
        return jnp.where(cnt >= cap, cand, thr)

    thr = lax.fori_loop(0, 31, search, jnp.zeros((1, ne), jnp.int32))
    gt = keys > thr
    eq = keys == thr
    ties_taken = (cap - jnp.sum(gt.astype(jnp.int32), axis=0, keepdims=True)).astype(F32)
    gt_ref[...] = jnp.where(gt, 1.0, 0.0).astype(BF16)
    eq_ref[...] = jnp.where(eq, 1.0, 0.0).astype(BF16)

    tri = (lax.broadcasted_iota(jnp.int32, (chunk, chunk), 0)
           >= lax.broadcasted_iota(jnp.int32, (chunk, chunk), 1)).astype(BF16)

    def prefix(ci, carry):
        rows = pl.ds(pl.multiple_of(ci * chunk, chunk), chunk)
        cg = jnp.dot(tri, gt_ref[rows, :], preferred_element_type=F32) + carry[0]
        ce = jnp.dot(tri, eq_ref[rows, :], preferred_element_type=F32) + carry[1]
        cgt_ref[rows, :] = cg
        ceq_ref[rows, :] = ce
        return cg[chunk - 1:chunk, :], ce[chunk - 1:chunk, :]

    zero = jnp.zeros((1, ne), F32)
    lax.fori_loop(0, t_len // chunk, prefix, (zero, zero))
    ceq = ceq_ref[...]
    sel = gt | (eq & (ceq <= ties_taken))
    csel = (cgt_ref[...] + jnp.minimum(ceq, ties_taken)).astype(jnp.int32)
    csel_ref[...] = csel
    pos_ref[0] = jnp.where(sel, csel - 1, -1)

    slot = lax.broadcasted_iota(jnp.int32, (1, cap), 1)
    for e in range(ne):
        def count(ci, acc):
            rows = pl.ds(pl.multiple_of(ci * chunk, chunk), chunk)
            le = (csel_ref[rows, e:e + 1] <= slot).astype(jnp.int32)
            return acc + jnp.sum(le.reshape(chunk // 8, 8, cap), axis=0)
        acc = lax.fori_loop(0, t_len // chunk, count, jnp.zeros((8, cap), jnp.int32))
        idx_ref[0, e:e + 1, :] = jnp.sum(acc, axis=0, keepdims=True)
    ends_ref[0] = csel_ref[pl.ds(BLOCK - 1, t_len // BLOCK, stride=BLOCK), :]


def _route(logits, cap):
    bsz, t_len, ne = logits.shape
    chunk = _pick((256, 128), t_len)
    nb = t_len // BLOCK
    return pl.pallas_call(
        functools.partial(_route_kernel, cap=cap, chunk=chunk),
        out_shape=(jax.ShapeDtypeStruct((bsz, ne, cap), jnp.int32),
                   jax.ShapeDtypeStruct((bsz, t_len, ne), jnp.int32),
                   jax.ShapeDtypeStruct((bsz, t_len, ne), F32),
                   jax.ShapeDtypeStruct((bsz, nb, ne), jnp.int32)),
        grid=(bsz,),
        in_specs=[pl.BlockSpec((1, t_len, ne), lambda b: (b, 0, 0))],
        out_specs=(pl.BlockSpec((1, ne, cap), lambda b: (b, 0, 0)),
                   pl.BlockSpec((1, t_len, ne), lambda b: (b, 0, 0)),
                   pl.BlockSpec((1, t_len, ne), lambda b: (b, 0, 0)),
                   pl.BlockSpec((1, nb, ne), lambda b: (b, 0, 0))),
        scratch_shapes=[pltpu.VMEM((t_len, ne), BF16), pltpu.VMEM((t_len, ne), BF16),
                        pltpu.VMEM((t_len, ne), F32), pltpu.VMEM((t_len, ne), F32),
                        pltpu.VMEM((t_len, ne), jnp.int32)],
        compiler_params=_cparams(("parallel",)),
        name="expert_choice_route",
    )(logits)


def _expert_kernel(rows_ref, hm_hbm, wg_ref, wu_ref, wd_ref, y_ref, xg_ref, hmid_ref, sem, *, cap, n_f, tf):
    e = pl.program_id(0)
    b = pl.program_id(1)
    j = pl.program_id(2)
    half = xg_ref.shape[1]

    def row_copy(r):
        src = rows_ref[e, b * cap + r]
        return pltpu.make_async_copy(hm_hbm.at[pl.ds(src, 1), :], xg_ref.at[pl.ds(r, 1), :], sem)

    @pl.when(j == 0)
    def _():
        def start(r, c):
            row_copy(r).start()
            return c

        def wait(r, c):
            row_copy(r).wait()
            return c

        lax.fori_loop(0, cap, start, 0)
        lax.fori_loop(0, cap, wait, 0)

    @pl.when(j < n_f)
    def _():
        x_lo, x_hi = _unpack_halves(xg_ref[...])
        a = (jnp.dot(x_lo, wg_ref[0, :half, :], preferred_element_type=F32)
             + jnp.dot(x_hi, wg_ref[0, half:, :], preferred_element_type=F32))
        u = (jnp.dot(x_lo, wu_ref[0, :half, :], preferred_element_type=F32)
             + jnp.dot(x_hi, wu_ref[0, half:, :], preferred_element_type=F32))
        hval = (a * _sigmoid(a) * u).astype(BF16)
        for jj in range(n_f):
            @pl.when(j == jj)
            def _():
                hmid_ref[:, jj * tf:(jj + 1) * tf] = hval

    @pl.when(j >= n_f)
    def _():
        y = jnp.dot(hmid_ref[...], wd_ref[0], preferred_element_type=F32)
        y_ref[0, 0, :cap, :] = y.astype(y_ref.dtype)
        y_ref[0, 0, cap:, :] = jnp.zeros((y_ref.shape[2] - cap, y_ref.shape[3]), y_ref.dtype)


def _experts(rows, hm_packed, wg, wu, wd, bsz, cap):
    ne, d, ff = wg.shape
    tf = _pick((512, 256, 128), ff)
    td = _pick((512, 256, 128), d)
    n_f = ff // tf
    n_d = d // td
    cp = cap + COMBINE_WIN
    grid_spec = pltpu.PrefetchScalarGridSpec(
        num_scalar_prefetch=1,
        grid=(ne, bsz, n_f + n_d),
        in_specs=[pl.BlockSpec(memory_space=pl.ANY),
                  pl.BlockSpec((1, d, tf), lambda e, b, j, *_: (e, 0, jnp.minimum(j, n_f - 1))),
                  pl.BlockSpec((1, d, tf), lambda e, b, j, *_: (e, 0, jnp.minimum(j, n_f - 1))),
                  pl.BlockSpec((1, ff, td), lambda e, b, j, *_: (e, 0, jnp.maximum(j - n_f, 0)))],
        out_specs=pl.BlockSpec((1, 1, cp, td), lambda e, b, j, *_: (e, b, 0, jnp.maximum(j - n_f, 0))),
        scratch_shapes=[pltpu.VMEM((cap, d // 2), jnp.uint32), pltpu.VMEM((cap, ff), BF16),
                        pltpu.SemaphoreType.DMA(())],
    )
    return pl.pallas_call(
        functools.partial(_expert_kernel, cap=cap, n_f=n_f, tf=tf),
        out_shape=jax.ShapeDtypeStruct((ne, bsz, cp, d), BF16),
        grid_spec=grid_spec,
        compiler_params=_cparams(("arbitrary", "arbitrary", "arbitrary")),
        name="expert_ffn",
    )(rows, hm_packed, wg, wu, wd)


def _combine_kernel(start_ref, x2_ref, pos_ref, aff_ref, g_ref, y_hbm, o_ref, buf_ref, sem, *, ne, nb, bsz):
    b = pl.program_id(0)
    j = pl.program_id(1)
    step = b * nb + j

    def window_copies(st, slot):
        bb = st // nb
        jj = st - bb * nb
        return [pltpu.make_async_copy(
            y_hbm.at[e, bb, pl.ds(pl.multiple_of(start_ref[(bb * ne + e) * nb + jj], BF16_SUBLANES), COMBINE_WIN), :],
            buf_ref.at[slot, e], sem.at[slot]) for e in range(ne)]

    @pl.when(step == 0)
    def _():
        for c in window_copies(step, 0):
            c.start()

    @pl.when(step + 1 < bsz * nb)
    def _():
        for c in window_copies(step + 1, (step + 1) % 2):
            c.start()

    slot = step % 2
    for c in window_copies(step, slot):
        c.wait()

    pos = pos_ref[0]
    gate = aff_ref[0]
    lane = lax.broadcasted_iota(jnp.int32, (1, COMBINE_WIN), 1)
    parts = []
    for e in range(ne):
        rel = pos[:, e:e + 1] - start_ref[(b * ne + e) * nb + j]
        parts.append(jnp.where(rel == lane, gate[:, e:e + 1], 0.0).astype(BF16))
    sel = jnp.concatenate(parts, axis=1)
    wins = buf_ref[slot].reshape(ne * COMBINE_WIN, buf_ref.shape[3])
    x3 = x2_ref[...] + jnp.dot(sel, wins, preferred_element_type=F32)
    o_ref[...] = _rms(x3, g_ref[...])


def _combine(starts, x2, pos, aff, g_final, y, seq):
    m, d = x2.shape
    ne, bsz = y.shape[0], y.shape[1]
    nb = seq // BLOCK
    grid_spec = pltpu.PrefetchScalarGridSpec(
        num_scalar_prefetch=1,
        grid=(bsz, nb),
        in_specs=[pl.BlockSpec((BLOCK, d), lambda b, j, *_: (b * nb + j, 0)),
                  pl.BlockSpec((1, BLOCK, ne), lambda b, j, *_: (b, j, 0)),
                  pl.BlockSpec((1, BLOCK, ne), lambda b, j, *_: (b, j, 0)),
                  pl.BlockSpec((1, d), lambda b, j, *_: (0, 0)),
                  pl.BlockSpec(memory_space=pl.ANY)],
        out_specs=pl.BlockSpec((BLOCK, d), lambda b, j, *_: (b * nb + j, 0)),
        scratch_shapes=[pltpu.VMEM((2, ne, COMBINE_WIN, d), BF16), pltpu.SemaphoreType.DMA((2,))],
    )
    return pl.pallas_call(
        functools.partial(_combine_kernel, ne=ne, nb=nb, bsz=bsz),
        out_shape=jax.ShapeDtypeStruct((m, d), F32),
        grid_spec=grid_spec,
        compiler_params=_cparams(("arbitrary", "arbitrary")),
        name="moe_combine_final_norm",
    )(starts, x2, pos, aff, g_final.reshape(1, d).astype(F32), y)


def kernel(x, mem, g_mix, w_in, conv_w, attn_sinks, w_conv_out, w_attn_out, w_out, g_cross, g_mem, w_q_cross,
           w_kv_cross, w_o_cross, g_moe, w_router, w_gate_e, w_up_e, w_down_e, g_final):
    bsz, seq, d = x.shape
    m = bsz * seq
    cw = conv_w.shape[1]
    aw = w_attn_out.shape[0]
    n_heads = attn_sinks.shape[0]
    in_cols = w_in.shape[1]
    kvw = (in_cols - 3 * cw - aw - 2 * d) // 2
    off_b, off_c, off_u = 0, cw, 2 * cw
    off_q = 3 * cw
    off_k = off_q + aw
    off_v = off_k + kvw
    off_ga = off_v + kvw
    off_gb = off_ga + d
    ne = w_router.shape[1]
    cap = CAPACITY_FACTOR * seq // ne

    x2d = x.reshape(m, d)

    h = _rmsnorm(x2d, g_mix, BF16)
    p = _inproj(h, w_in.astype(BF16), off_ga)
    za = _conv_mixer(p, conv_w, seq, off_b, off_c, off_u)
    slopes = jnp.power(2.0, -8.0 * (jnp.arange(n_heads, dtype=F32) + 1.0) / n_heads)
    ob = _windowed_attention(p, slopes, attn_sinks.astype(F32), bsz, seq, off_q, off_k, off_v, aw, kvw, n_heads)
    mix = _merge(za, ob, w_conv_out.astype(BF16), w_attn_out.astype(BF16), p, off_ga, off_gb)
    x1 = _resid_matmul(mix, w_out.astype(BF16), x2d)

    mem_len = mem.shape[1]
    kv = _mem_kv(mem.reshape(bsz * mem_len, d), g_mem, w_kv_cross.astype(BF16)).reshape(bsz, mem_len, -1)
    wr_hi = w_router.astype(BF16)
    wr_lo = (w_router - wr_hi.astype(F32)).astype(BF16)
    x2, hm_packed, logits = _cross_attention(x1, g_cross, w_q_cross.astype(BF16), kv, w_o_cross.astype(BF16),
                                             g_moe, wr_hi, wr_lo, seq)

    idx, pos, aff, ends = _route(logits.reshape(bsz, seq, ne), cap)
    rows = (idx + (jnp.arange(bsz, dtype=jnp.int32) * seq)[:, None, None]).transpose(1, 0, 2).reshape(ne, bsz * cap)
    y = _experts(rows, hm_packed, w_gate_e.astype(BF16), w_up_e.astype(BF16), w_down_e.astype(BF16), bsz, cap)
    first = jnp.concatenate([jnp.zeros((bsz, 1, ne), jnp.int32), ends[:, :-1, :]], axis=1)
    starts = ((first // BF16_SUBLANES) * BF16_SUBLANES).transpose(0, 2, 1).reshape(-1)
    out = _combine(starts, x2, pos, aff, g_final, y, seq)
    return out.reshape(bsz, seq, d)
```

```python
import functools
import math

import jax
import jax.numpy as jnp
from jax import lax
from jax.experimental import pallas as pl
from jax.experimental.pallas import tpu as pltpu

RMS_EPS = 1e-6
NEG_INF = -1e30
WINDOW = 128
BLOCK = 128
N_CROSS_HEADS = 4
CAPACITY_FACTOR = 2
BF16_SUBLANES = 16
COMBINE_WIN = BLOCK + BF16_SUBLANES
VMEM_LIMIT_BYTES = 56 * 1024 * 1024

F32 = jnp.float32
BF16 = jnp.bfloat16


def _cparams(semantics, vmem=VMEM_LIMIT_BYTES):
    return pltpu.CompilerParams(dimension_semantics=semantics, vmem_limit_bytes=vmem)


def _pick(prefs, *sizes):
    for t in prefs:
        if all(s % t == 0 for s in sizes):
            return t
    raise ValueError(f"no tile in {prefs} divides {sizes}")


def _rms(x, g):
    return x * lax.rsqrt(jnp.mean(x * x, axis=-1, keepdims=True) + RMS_EPS) * g


def _sigmoid(x):
    return 1.0 / (1.0 + jnp.exp(-x))


def _rmsnorm_kernel(x_ref, g_ref, o_ref):
    o_ref[...] = _rms(x_ref[...].astype(F32), g_ref[...]).astype(o_ref.dtype)


def _rmsnorm(x2d, g, out_dtype):
    m, d = x2d.shape
    tm = _pick((256, 128, 64, 8), m)
    return pl.pallas_call(
        _rmsnorm_kernel,
        out_shape=jax.ShapeDtypeStruct((m, d), out_dtype),
        grid=(m // tm,),
        in_specs=[pl.BlockSpec((tm, d), lambda i: (i, 0)), pl.BlockSpec((1, d), lambda i: (0, 0))],
        out_specs=pl.BlockSpec((tm, d), lambda i: (i, 0)),
        compiler_params=_cparams(("parallel",)),
        name="rmsnorm",
    )(x2d, g.reshape(1, d).astype(F32))


def _inproj_kernel(h_ref, w_ref, o_ref, wb_ref, *, gate_tile0):
    j = pl.program_id(0)

    @pl.when(pl.program_id(1) == 0)
    def _():
        wb_ref[...] = w_ref[...].astype(BF16)

    acc = jnp.dot(h_ref[...], wb_ref[...], preferred_element_type=F32)

    @pl.when(j < gate_tile0)
    def _():
        o_ref[...] = acc.astype(o_ref.dtype)

    @pl.when(j >= gate_tile0)
    def _():
        o_ref[...] = _sigmoid(acc).astype(o_ref.dtype)


def _inproj(h, w, off_gate):
    m, d = h.shape
    n = w.shape[1]
    tm = _pick((1024, 512, 256, 128), m)
    tn = _pick((512, 256, 128), n, off_gate)
    return pl.pallas_call(
        functools.partial(_inproj_kernel, gate_tile0=off_gate // tn),
        out_shape=jax.ShapeDtypeStruct((m, n), BF16),
        grid=(n // tn, m // tm),
        in_specs=[pl.BlockSpec((tm, d), lambda j, i: (i, 0)), pl.BlockSpec((d, tn), lambda j, i: (0, j))],
        out_specs=pl.BlockSpec((tm, tn), lambda j, i: (i, j)),
        scratch_shapes=[pltpu.VMEM((d, tn), BF16)],
        compiler_params=_cparams(("arbitrary", "arbitrary")),
        name="inproj",
    )(h, w)


def _conv_kernel(b_ref, c_ref, u_ref, cp_ref, up_ref, cn_ref, un_ref, w_ref, o_ref, *, seq_blocks):
    i = pl.program_id(0)
    tm = c_ref.shape[0]
    z = c_ref[...].astype(F32) * u_ref[...].astype(F32)
    last_row = BF16_SUBLANES - 1
    z_prev = cp_ref[last_row:last_row + 1, :].astype(F32) * up_ref[last_row:last_row + 1, :].astype(F32)
    z_next = cn_ref[0:1, :].astype(F32) * un_ref[0:1, :].astype(F32)
    pos = i % seq_blocks
    z_prev = jnp.where(pos == 0, 0.0, z_prev)
    z_next = jnp.where(pos == seq_blocks - 1, 0.0, z_next)
    row = lax.broadcasted_iota(jnp.int32, z.shape, 0)
    z_up = jnp.where(row == 0, z_prev, pltpu.roll(z, 1, axis=0))
    z_dn = jnp.where(row == tm - 1, z_next, pltpu.roll(z, tm - 1, axis=0))
    w = w_ref[...]
    conv = w[0:1, :] * z_up + w[1:2, :] * z + w[2:3, :] * z_dn
    o_ref[...] = (b_ref[...].astype(F32) * conv).astype(o_ref.dtype)


def _conv_mixer(p, conv_w, seq, off_b, off_c, off_u):
    m = p.shape[0]
    cw = conv_w.shape[1]
    tm = _pick((512, 256, 128), seq)
    tc = _pick((512, 256, 128), cw, off_b, off_c, off_u)
    halo = BF16_SUBLANES
    rb = tm // halo
    n_halo = m // halo

    def main(off):
        return pl.BlockSpec((tm, tc), lambda i, j: (i, off // tc + j))

    def prev(off):
        return pl.BlockSpec((halo, tc), lambda i, j: (jnp.maximum(i * rb - 1, 0), off // tc + j))

    def nxt(off):
        return pl.BlockSpec((halo, tc), lambda i, j: (jnp.minimum((i + 1) * rb, n_halo - 1), off // tc + j))

    return pl.pallas_call(
        functools.partial(_conv_kernel, seq_blocks=seq // tm),
        out_shape=jax.ShapeDtypeStruct((m, cw), BF16),
        grid=(m // tm, cw // tc),
        in_specs=[main(off_b), main(off_c), main(off_u), prev(off_c), prev(off_u), nxt(off_c), nxt(off_u),
                  pl.BlockSpec((conv_w.shape[0], tc), lambda i, j: (0, j))],
        out_specs=pl.BlockSpec((tm, tc), lambda i, j: (i, j)),
        compiler_params=_cparams(("parallel", "parallel")),
        name="conv_mixer",
    )(p, p, p, p, p, p, p, conv_w.astype(F32))


def _swa_kernel(slopes_ref, sinks_ref, q_ref, kp_ref, kc_ref, kn_ref, vp_ref, vc_ref, vn_ref, o_ref,
                *, n_kv, group, hd, nb):
    n = pl.program_id(1)
    has_prev = n > 0
    has_next = n < nb - 1
    shape = (BLOCK, 3 * BLOCK)
    krel = lax.broadcasted_iota(jnp.int32, shape, 1) - BLOCK
    dist = jnp.abs(lax.broadcasted_iota(jnp.int32, shape, 0) - krel)
    valid = (dist <= WINDOW) & ((krel >= 0) | has_prev) & ((krel < BLOCK) | has_next)
    distf = dist.astype(F32)
    scale = 1.0 / math.sqrt(hd)
    for kh in range(n_kv):
        cols = slice(kh * hd, (kh + 1) * hd)
        kband = jnp.concatenate([kp_ref[:, cols], kc_ref[:, cols], kn_ref[:, cols]], axis=0)
        vband = jnp.concatenate([vp_ref[:, cols], vc_ref[:, cols], vn_ref[:, cols]], axis=0)
        for gi in range(group):
            h = kh * group + gi
            hcols = slice(h * hd, (h + 1) * hd)
            s = lax.dot_general(q_ref[:, hcols], kband, (((1,), (1,)), ((), ())),
                                preferred_element_type=F32) * scale
            s = jnp.where(valid, s - slopes_ref[h] * distf, NEG_INF)
            sink = sinks_ref[h]
            mx = jnp.maximum(jnp.max(s, axis=-1, keepdims=True), sink)
            pr = jnp.exp(s - mx)
            denom = jnp.sum(pr, axis=-1, keepdims=True) + jnp.exp(sink - mx)
            o = jnp.dot(pr.astype(BF16), vband, preferred_element_type=F32) / denom
            o_ref[:, hcols] = o.astype(o_ref.dtype)


def _windowed_attention(p, slopes, sinks, bsz, seq, off_q, off_k, off_v, aw, kvw, n_heads):
    m = p.shape[0]
    hd = aw // n_heads
    n_kv = kvw // hd
    nb = seq // BLOCK
    assert off_q % aw == 0 and off_k % kvw == 0 and off_v % kvw == 0 and seq % BLOCK == 0

    def kv_spec(off, shift):
        def imap(b, n, *_):
            return (b * nb + jnp.clip(n + shift, 0, nb - 1), off // kvw)
        return pl.BlockSpec((BLOCK, kvw), imap)

    grid_spec = pltpu.PrefetchScalarGridSpec(
        num_scalar_prefetch=2,
        grid=(bsz, nb),
        in_specs=[pl.BlockSpec((BLOCK, aw), lambda b, n, *_: (b * nb + n, off_q // aw)),
                  kv_spec(off_k, -1), kv_spec(off_k, 0), kv_spec(off_k, 1),
                  kv_spec(off_v, -1), kv_spec(off_v, 0), kv_spec(off_v, 1)],
        out_specs=pl.BlockSpec((BLOCK, aw), lambda b, n, *_: (b * nb + n, 0)),
    )
    return pl.pallas_call(
        functools.partial(_swa_kernel, n_kv=n_kv, group=n_heads // n_kv, hd=hd, nb=nb),
        out_shape=jax.ShapeDtypeStruct((m, aw), BF16),
        grid_spec=grid_spec,
        compiler_params=_cparams(("parallel", "parallel")),
        name="windowed_gqa",
    )(slopes, sinks, p, p, p, p, p, p, p)


def _merge_kernel(za_ref, ob_ref, wc_ref, wa_ref, ga_ref, gb_ref, o_ref, wcb_ref, wab_ref):
    @pl.when(pl.program_id(1) == 0)
    def _():
        wcb_ref[...] = wc_ref[...].astype(BF16)
        wab_ref[...] = wa_ref[...].astype(BF16)

    ya = jnp.dot(za_ref[...], wcb_ref[...], preferred_element_type=F32)
    yb = jnp.dot(ob_ref[...], wab_ref[...], preferred_element_type=F32)
    o_ref[...] = (ga_ref[...].astype(F32) * ya + gb_ref[...].astype(F32) * yb).astype(o_ref.dtype)


def _merge(za, ob, wc, wa, p, off_ga, off_gb):
    m, ka = za.shape
    kb = ob.shape[1]
    d = wc.shape[1]
    tm = _pick((1024, 512, 256, 128), m)
    tn = _pick((512, 256, 128), d, off_ga, off_gb)
    return pl.pallas_call(
        _merge_kernel,
        out_shape=jax.ShapeDtypeStruct((m, d), BF16),
        grid=(d // tn, m // tm),
        in_specs=[pl.BlockSpec((tm, ka), lambda j, i: (i, 0)), pl.BlockSpec((tm, kb), lambda j, i: (i, 0)),
                  pl.BlockSpec((ka, tn), lambda j, i: (0, j)), pl.BlockSpec((kb, tn), lambda j, i: (0, j)),
                  pl.BlockSpec((tm, tn), lambda j, i: (i, off_ga // tn + j)),
                  pl.BlockSpec((tm, tn), lambda j, i: (i, off_gb // tn + j))],
        out_specs=pl.BlockSpec((tm, tn), lambda j, i: (i, j)),
        scratch_shapes=[pltpu.VMEM((ka, tn), BF16), pltpu.VMEM((kb, tn), BF16)],
        compiler_params=_cparams(("arbitrary", "arbitrary")),
        name="gated_merge",
    )(za, ob, wc, wa, p, p)


def _resid_matmul_kernel(a_ref, w_ref, x_ref, o_ref, wb_ref):
    @pl.when(pl.program_id(1) == 0)
    def _():
        wb_ref[...] = w_ref[...].astype(BF16)

    o_ref[...] = x_ref[...] + jnp.dot(a_ref[...], wb_ref[...], preferred_element_type=F32)


def _resid_matmul(a, w, x):
    m, k = a.shape
    n = w.shape[1]
    tm = _pick((1024, 512, 256, 128), m)
    tn = _pick((512, 256, 128), n)
    return pl.pallas_call(
        _resid_matmul_kernel,
        out_shape=jax.ShapeDtypeStruct((m, n), F32),
        grid=(n // tn, m // tm),
        in_specs=[pl.BlockSpec((tm, k), lambda j, i: (i, 0)), pl.BlockSpec((k, tn), lambda j, i: (0, j)),
                  pl.BlockSpec((tm, tn), lambda j, i: (i, j))],
        out_specs=pl.BlockSpec((tm, tn), lambda j, i: (i, j)),
        scratch_shapes=[pltpu.VMEM((k, tn), BF16)],
        compiler_params=_cparams(("arbitrary", "arbitrary")),
        name="out_proj_residual",
    )(a, w, x)


def _mem_kv_kernel(mem_ref, g_ref, w_ref, o_ref):
    mn = _rms(mem_ref[...].astype(F32), g_ref[...]).astype(BF16)
    o_ref[...] = jnp.dot(mn, w_ref[...], preferred_element_type=F32).astype(o_ref.dtype)


def _mem_kv(mem2d, g, w):
    m, d = mem2d.shape
    n = w.shape[1]
    tn = _pick((512, 256, 128), n)
    return pl.pallas_call(
        _mem_kv_kernel,
        out_shape=jax.ShapeDtypeStruct((m, n), BF16),
        grid=(n // tn,),
        in_specs=[pl.BlockSpec((m, d), lambda j: (0, 0)), pl.BlockSpec((1, d), lambda j: (0, 0)),
                  pl.BlockSpec((d, tn), lambda j: (0, j))],
        out_specs=pl.BlockSpec((m, tn), lambda j: (0, j)),
        compiler_params=_cparams(("parallel",)),
        name="mem_kv",
    )(mem2d, g.reshape(1, d).astype(F32), w)


def _pack_halves(hb):
    half = hb.shape[1] // 2
    lo = lax.bitcast_convert_type(hb[:, :half].astype(F32), jnp.uint32)
    hi = lax.bitcast_convert_type(hb[:, half:].astype(F32), jnp.uint32)
    return (lo >> 16) | (hi & jnp.uint32(0xFFFF0000))


def _unpack_halves(w):
    lo = lax.bitcast_convert_type(w << 16, F32).astype(BF16)
    hi = lax.bitcast_convert_type(w & jnp.uint32(0xFFFF0000), F32).astype(BF16)
    return lo, hi


def _cross_kernel(x_ref, gc_ref, wq_ref, kv_ref, wo_ref, gm_ref, wrh_ref, wrl_ref,
                  x2_ref, hm_ref, lg_ref, *, cw):
    x = x_ref[...]
    hc = _rms(x, gc_ref[...]).astype(BF16)
    q = jnp.dot(hc, wq_ref[...], preferred_element_type=F32).astype(BF16)
    hd = cw // N_CROSS_HEADS
    inv = 1.0 / math.sqrt(hd)
    outs = []
    for h in range(N_CROSS_HEADS):
        k = kv_ref[0, :, h * hd:(h + 1) * hd]
        v = kv_ref[0, :, cw + h * hd:cw + (h + 1) * hd]
        s = lax.dot_general(q[:, h * hd:(h + 1) * hd], k, (((1,), (1,)), ((), ())),
                            preferred_element_type=F32) * inv
        pr = jnp.exp(s - jnp.max(s, axis=-1, keepdims=True))
        o = jnp.dot(pr.astype(BF16), v, preferred_element_type=F32) / jnp.sum(pr, axis=-1, keepdims=True)
        outs.append(o.astype(BF16))
    o = jnp.concatenate(outs, axis=1)
    x2 = x + jnp.dot(o, wo_ref[...], preferred_element_type=F32)
    x2_ref[...] = x2
    hm = _rms(x2, gm_ref[...])
    hi = hm.astype(BF16)
    lo = (hm - hi.astype(F32)).astype(BF16)
    hm_ref[...] = _pack_halves(hi)
    lg_ref[...] = (jnp.dot(hi, wrh_ref[...], preferred_element_type=F32)
                   + jnp.dot(lo, wrh_ref[...], preferred_element_type=F32)
                   + jnp.dot(hi, wrl_ref[...], preferred_element_type=F32))


def _cross_attention(x1, g_cross, wq, kv, wo, g_moe, wr_hi, wr_lo, seq):
    m, d = x1.shape
    cw = wq.shape[1]
    ne = wr_hi.shape[1]
    mem_len = kv.shape[1]
    tm = _pick((256, 128), seq)
    per_seq = seq // tm
    const = lambda i: (0, 0)
    return pl.pallas_call(
        functools.partial(_cross_kernel, cw=cw),
        out_shape=(jax.ShapeDtypeStruct((m, d), F32), jax.ShapeDtypeStruct((m, d // 2), jnp.uint32),
                   jax.ShapeDtypeStruct((m, ne), F32)),
        grid=(m // tm,),
        in_specs=[pl.BlockSpec((tm, d), lambda i: (i, 0)), pl.BlockSpec((1, d), const),
                  pl.BlockSpec((d, cw), const),
                  pl.BlockSpec((1, mem_len, 2 * cw), lambda i: (i // per_seq, 0, 0)),
                  pl.BlockSpec((cw, d), const), pl.BlockSpec((1, d), const),
                  pl.BlockSpec((d, ne), const), pl.BlockSpec((d, ne), const)],
        out_specs=(pl.BlockSpec((tm, d), lambda i: (i, 0)), pl.BlockSpec((tm, d // 2), lambda i: (i, 0)),
                   pl.BlockSpec((tm, ne), lambda i: (i, 0))),
        compiler_params=_cparams(("parallel",)),
        name="cross_attention",
    )(x1, g_cross.reshape(1, d).astype(F32), wq, kv, wo, g_moe.reshape(1, d).astype(F32), wr_hi, wr_lo)


def _route_kernel(lg_ref, idx_ref, pos_ref, aff_ref, ends_ref, gt_ref, eq_ref, cgt_ref, ceq_ref, csel_ref,
                  *, cap, chunk):
    logits = lg_ref[0]
    t_len, ne = logits.shape
    ex = jnp.exp(logits - jnp.max(logits, axis=-1, keepdims=True))
    aff = ex / jnp.sum(ex, axis=-1, keepdims=True)
    aff_ref[0] = aff
    keys = lax.bitcast_convert_type(aff, jnp.int32)

    def search(it, thr):
        cand = thr | jnp.left_shift(jnp.int32(1), 30 - it)
        cnt = jnp.sum((keys >= cand).astype(jnp.int32), axis=0, keepdims=True)
        return jnp.where(cnt >= cap, cand, thr)

    thr = lax.fori_loop(0, 31, search, jnp.zeros((1, ne), jnp.int32))
    gt = keys > thr
    eq = keys == thr
    ties_taken = (cap - jnp.sum(gt.astype(jnp.int32), axis=0, keepdims=True)).astype(F32)
    gt_ref[...] = jnp.where(gt, 1.0, 0.0).astype(BF16)
    eq_ref[...] = jnp.where(eq, 1.0, 0.0).astype(BF16)

    tri = (lax.broadcasted_iota(jnp.int32, (chunk, chunk), 0)
           >= lax.broadcasted_iota(jnp.int32, (chunk, chunk), 1)).astype(BF16)

    def prefix(ci, carry):
        rows = pl.ds(pl.multiple_of(ci * chunk, chunk), chunk)
        cg = jnp.dot(tri, gt_ref[rows, :], preferred_element_type=F32) + carry[0]
        ce = jnp.dot(tri, eq_ref[rows, :], preferred_element_type=F32) + carry[1]
        cgt_ref[rows, :] = cg
        ceq_ref[rows, :] = ce
        return cg[chunk - 1:chunk, :], ce[chunk - 1:chunk, :]

    zero = jnp.zeros((1, ne), F32)
    lax.fori_loop(0, t_len // chunk, prefix, (zero, zero))
    ceq = ceq_ref[...]
    sel = gt | (eq & (ceq <= ties_taken))
    csel = (cgt_ref[...] + jnp.minimum(ceq, ties_taken)).astype(jnp.int32)
    csel_ref[...] = csel
    pos_ref[0] = jnp.where(sel, csel - 1, -1)

    slot = lax.broadcasted_iota(jnp.int32, (1, cap), 1)
    for e in range(ne):
        def count(ci, acc):
            rows = pl.ds(pl.multiple_of(ci * chunk, chunk), chunk)
            le = (csel_ref[rows, e:e + 1] <= slot).astype(jnp.int32)
            return acc + jnp.sum(le.reshape(chunk // 8, 8, cap), axis=0)
        acc = lax.fori_loop(0, t_len // chunk, count, jnp.zeros((8, cap), jnp.int32))
        idx_ref[0, e:e + 1, :] = jnp.sum(acc, axis=0, keepdims=True)
    ends_ref[0] = csel_ref[pl.ds(BLOCK - 1, t_len // BLOCK, stride=BLOCK), :]


def _route(logits, cap):
    bsz, t_len, ne = logits.shape
    chunk = _pick((256, 128), t_len)
    nb = t_len // BLOCK
    return pl.pallas_call(
        functools.partial(_route_kernel, cap=cap, chunk=chunk),
        out_shape=(jax.ShapeDtypeStruct((bsz, ne, cap), jnp.int32),
                   jax.ShapeDtypeStruct((bsz, t_len, ne), jnp.int32),
                   jax.ShapeDtypeStruct((bsz, t_len, ne), F32),
                   jax.ShapeDtypeStruct((bsz, nb, ne), jnp.int32)),
        grid=(bsz,),
        in_specs=[pl.BlockSpec((1, t_len, ne), lambda b: (b, 0, 0))],
        out_specs=(pl.BlockSpec((1, ne, cap), lambda b: (b, 0, 0)),
                   pl.BlockSpec((1, t_len, ne), lambda b: (b, 0, 0)),
                   pl.BlockSpec((1, t_len, ne), lambda b: (b, 0, 0)),
                   pl.BlockSpec((1, nb, ne), lambda b: (b, 0, 0))),
        scratch_shapes=[pltpu.VMEM((t_len, ne), BF16), pltpu.VMEM((t_len, ne), BF16),
                        pltpu.VMEM((t_len, ne), F32), pltpu.VMEM((t_len, ne), F32),
                        pltpu.VMEM((t_len, ne), jnp.int32)],
        compiler_params=_cparams(("parallel",)),
        name="expert_choice_route",
    )(logits)


def _expert_kernel(rows_ref, hm_hbm, wg_ref, wu_ref, wd_ref, y_ref, xg_ref, xb_ref, hmid_ref, sem,
                   *, cap, n_f, tf, n_groups):
    j = pl.program_id(2)
    group = pl.program_id(0) * pl.num_programs(1) + pl.program_id(1)
    half = xg_ref.shape[1]

    def row_copy(g, r):
        return pltpu.make_async_copy(hm_hbm.at[pl.ds(rows_ref[g * cap + r], 1), :],
                                     xg_ref.at[pl.ds(r, 1), :], sem)

    def start_gather(g):
        def body(r, c):
            row_copy(g, r).start()
            return c
        lax.fori_loop(0, cap, body, 0)

    def wait_gather(g):
        def body(r, c):
            row_copy(g, r).wait()
            return c
        lax.fori_loop(0, cap, body, 0)

    @pl.when(j == 0)
    def _():
        @pl.when(group == 0)
        def _():
            start_gather(group)

        wait_gather(group)
        lo, hi = _unpack_halves(xg_ref[...])
        xb_ref[:, :half] = lo
        xb_ref[:, half:] = hi

    @pl.when(j == n_f)
    def _():
        @pl.when(group + 1 < n_groups)
        def _():
            start_gather(group + 1)

    @pl.when(j < n_f)
    def _():
        x = xb_ref[...]
        a = jnp.dot(x, wg_ref[0].astype(BF16), preferred_element_type=F32)
        u = jnp.dot(x, wu_ref[0].astype(BF16), preferred_element_type=F32)
        hval = (a * _sigmoid(a) * u).astype(BF16)
        for jj in range(n_f):
            @pl.when(j == jj)
            def _():
                hmid_ref[:, jj * tf:(jj + 1) * tf] = hval

    @pl.when(j >= n_f)
    def _():
        y = jnp.dot(hmid_ref[...], wd_ref[0].astype(BF16), preferred_element_type=F32)
        y_ref[0, 0, :cap, :] = y.astype(y_ref.dtype)
        y_ref[0, 0, cap:, :] = jnp.zeros((y_ref.shape[2] - cap, y_ref.shape[3]), y_ref.dtype)


def _experts(rows, hm_packed, wg, wu, wd, bsz, cap):
    ne, d, ff = wg.shape
    tf = _pick((256, 128), ff)
    td = _pick((256, 128), d)
    n_f = ff // tf
    n_d = d // td
    cp = cap + COMBINE_WIN
    grid_spec = pltpu.PrefetchScalarGridSpec(
        num_scalar_prefetch=1,
        grid=(ne, bsz, n_f + n_d),
        in_specs=[pl.BlockSpec(memory_space=pl.ANY),
                  pl.BlockSpec((1, d, tf), lambda e, b, j, *_: (e, 0, jnp.minimum(j, n_f - 1))),
                  pl.BlockSpec((1, d, tf), lambda e, b, j, *_: (e, 0, jnp.minimum(j, n_f - 1))),
                  pl.BlockSpec((1, ff, td), lambda e, b, j, *_: (e, 0, jnp.maximum(j - n_f, 0)))],
        out_specs=pl.BlockSpec((1, 1, cp, td), lambda e, b, j, *_: (e, b, 0, jnp.maximum(j - n_f, 0))),
        scratch_shapes=[pltpu.VMEM((cap, d // 2), jnp.uint32), pltpu.VMEM((cap, d), BF16),
                        pltpu.VMEM((cap, ff), BF16), pltpu.SemaphoreType.DMA(())],
    )
    return pl.pallas_call(
        functools.partial(_expert_kernel, cap=cap, n_f=n_f, tf=tf, n_groups=ne * bsz),
        out_shape=jax.ShapeDtypeStruct((ne, bsz, cp, d), BF16),
        grid_spec=grid_spec,
        compiler_params=_cparams(("arbitrary", "arbitrary", "arbitrary")),
        name="expert_ffn",
    )(rows, hm_packed, wg, wu, wd)


def _combine_kernel(start_ref, x2_ref, pos_ref, aff_ref, g_ref, y_hbm, o_ref, buf_ref, sem, *, ne, nb, bsz):
    b = pl.program_id(0)
    j = pl.program_id(1)
    step = b * nb + j

    def window_copies(st, slot):
        bb = st // nb
        jj = st - bb * nb
        return [pltpu.make_async_copy(
            y_hbm.at[e, bb, pl.ds(pl.multiple_of(start_ref[(bb * ne + e) * nb + jj], BF16_SUBLANES), COMBINE_WIN), :],
            buf_ref.at[slot, e], sem.at[slot]) for e in range(ne)]

    @pl.when(step == 0)
    def _():
        for c in window_copies(step, 0):
            c.start()

    @pl.when(step + 1 < bsz * nb)
    def _():
        for c in window_copies(step + 1, (step + 1) % 2):
            c.start()

    slot = step % 2
    for c in window_copies(step, slot):
        c.wait()

    pos = pos_ref[0]
    gate = aff_ref[0]
    lane = lax.broadcasted_iota(jnp.int32, (1, COMBINE_WIN), 1)
    parts = []
    for e in range(ne):
        rel = pos[:, e:e + 1] - start_ref[(b * ne + e) * nb + j]
        parts.append(jnp.where(rel == lane, gate[:, e:e + 1], 0.0).astype(BF16))
    sel = jnp.concatenate(parts, axis=1)
    wins = buf_ref[slot].reshape(ne * COMBINE_WIN, buf_ref.shape[3])
    x3 = x2_ref[...] + jnp.dot(sel, wins, preferred_element_type=F32)
    o_ref[...] = _rms(x3, g_ref[...])


def _combine(starts, x2, pos, aff, g_final, y, seq):
    m, d = x2.shape
    ne, bsz = y.shape[0], y.shape[1]
    nb = seq // BLOCK
    grid_spec = pltpu.PrefetchScalarGridSpec(
        num_scalar_prefetch=1,
        grid=(bsz, nb),
        in_specs=[pl.BlockSpec((BLOCK, d), lambda b, j, *_: (b * nb + j, 0)),
                  pl.BlockSpec((1, BLOCK, ne), lambda b, j, *_: (b, j, 0)),
                  pl.BlockSpec((1, BLOCK, ne), lambda b, j, *_: (b, j, 0)),
                  pl.BlockSpec((1, d), lambda b, j, *_: (0, 0)),
                  pl.BlockSpec(memory_space=pl.ANY)],
        out_specs=pl.BlockSpec((BLOCK, d), lambda b, j, *_: (b * nb + j, 0)),
        scratch_shapes=[pltpu.VMEM((2, ne, COMBINE_WIN, d), BF16), pltpu.SemaphoreType.DMA((2,))],
    )
    return pl.pallas_call(
        functools.partial(_combine_kernel, ne=ne, nb=nb, bsz=bsz),
        out_shape=jax.ShapeDtypeStruct((m, d), F32),
        grid_spec=grid_spec,
        compiler_params=_cparams(("arbitrary", "arbitrary")),
        name="moe_combine_final_norm",
    )(starts, x2, pos, aff, g_final.reshape(1, d).astype(F32), y)


def kernel(x, mem, g_mix, w_in, conv_w, attn_sinks, w_conv_out, w_attn_out, w_out, g_cross, g_mem, w_q_cross,
           w_kv_cross, w_o_cross, g_moe, w_router, w_gate_e, w_up_e, w_down_e, g_final):
    bsz, seq, d = x.shape
    m = bsz * seq
    cw = conv_w.shape[1]
    aw = w_attn_out.shape[0]
    n_heads = attn_sinks.shape[0]
    in_cols = w_in.shape[1]
    kvw = (in_cols - 3 * cw - aw - 2 * d) // 2
    off_b, off_c, off_u = 0, cw, 2 * cw
    off_q = 3 * cw
    off_k = off_q + aw
    off_v = off_k + kvw
    off_ga = off_v + kvw
    off_gb = off_ga + d
    ne = w_router.shape[1]
    cap = CAPACITY_FACTOR * seq // ne

    x2d = x.reshape(m, d)

    h = _rmsnorm(x2d, g_mix, BF16)
    p = _inproj(h, w_in, off_ga)
    za = _conv_mixer(p, conv_w, seq, off_b, off_c, off_u)
    slopes = jnp.power(2.0, -8.0 * (jnp.arange(n_heads, dtype=F32) + 1.0) / n_heads)
    ob = _windowed_attention(p, slopes, attn_sinks.astype(F32), bsz, seq, off_q, off_k, off_v, aw, kvw, n_heads)
    mix = _merge(za, ob, w_conv_out, w_attn_out, p, off_ga, off_gb)
    x1 = _resid_matmul(mix, w_out, x2d)

    mem_len = mem.shape[1]
    kv = _mem_kv(mem.reshape(bsz * mem_len, d), g_mem, w_kv_cross.astype(BF16)).reshape(bsz, mem_len, -1)
    wr_hi = w_router.astype(BF16)
    wr_lo = (w_router - wr_hi.astype(F32)).astype(BF16)
    x2, hm_packed, logits = _cross_attention(x1, g_cross, w_q_cross.astype(BF16), kv, w_o_cross.astype(BF16),
                                             g_moe, wr_hi, wr_lo, seq)

    idx, pos, aff, ends = _route(logits.reshape(bsz, seq, ne), cap)
    rows = (idx + (jnp.arange(bsz, dtype=jnp.int32) * seq)[:, None, None]).transpose(1, 0, 2).reshape(-1)
    y = _experts(rows, hm_packed, w_gate_e, w_up_e, w_down_e, bsz, cap)
    first = jnp.concatenate([jnp.zeros((bsz, 1, ne), jnp.int32), ends[:, :-1, :]], axis=1)
    starts = ((first // BF16_SUBLANES) * BF16_SUBLANES).transpose(0, 2, 1).reshape(-1)
    out = _combine(starts, x2, pos, aff, g_final, y, seq)
    return out.reshape(bsz, seq, d)
```

```python
import functools
import math

import jax
import jax.numpy as jnp
from jax import lax
from jax.experimental import pallas as pl
from jax.experimental.pallas import tpu as pltpu

RMS_EPS = 1e-6
NEG_INF = -1e30
WINDOW = 128
BLOCK = 128
N_CROSS_HEADS = 4
CAPACITY_FACTOR = 2
BF16_SUBLANES = 16
COMBINE_WIN = BLOCK + BF16_SUBLANES
COMBINE_WIN_SMALL = 3 * BF16_SUBLANES
VMEM_LIMIT_BYTES = 56 * 1024 * 1024

F32 = jnp.float32
BF16 = jnp.bfloat16


def _cparams(semantics, vmem=VMEM_LIMIT_BYTES):
    return pltpu.CompilerParams(dimension_semantics=semantics, vmem_limit_bytes=vmem)


def _pick(prefs, *sizes):
    for t in prefs:
        if all(s % t == 0 for s in sizes):
            return t
    raise ValueError(f"no tile in {prefs} divides {sizes}")


def _rms(x, g):
    return x * lax.rsqrt(jnp.mean(x * x, axis=-1, keepdims=True) + RMS_EPS) * g


def _sigmoid(x):
    return 1.0 / (1.0 + jnp.exp(-x))


def _rmsnorm_kernel(x_ref, g_ref, o_ref):
    o_ref[...] = _rms(x_ref[...].astype(F32), g_ref[...]).astype(o_ref.dtype)


def _rmsnorm(x2d, g, out_dtype):
    m, d = x2d.shape
    tm = _pick((256, 128, 64, 8), m)
    return pl.pallas_call(
        _rmsnorm_kernel,
        out_shape=jax.ShapeDtypeStruct((m, d), out_dtype),
        grid=(m // tm,),
        in_specs=[pl.BlockSpec((tm, d), lambda i: (i, 0)), pl.BlockSpec((1, d), lambda i: (0, 0))],
        out_specs=pl.BlockSpec((tm, d), lambda i: (i, 0)),
        compiler_params=_cparams(("parallel",)),
        name="rmsnorm",
    )(x2d, g.reshape(1, d).astype(F32))


def _inproj_kernel(h_ref, w_ref, o_ref, wb_ref, *, gate_tile0):
    j = pl.program_id(0)

    @pl.when(pl.program_id(1) == 0)
    def _():
        wb_ref[...] = w_ref[...].astype(BF16)

    acc = jnp.dot(h_ref[...], wb_ref[...], preferred_element_type=F32)

    @pl.when(j < gate_tile0)
    def _():
        o_ref[...] = acc.astype(o_ref.dtype)

    @pl.when(j >= gate_tile0)
    def _():
        o_ref[...] = _sigmoid(acc).astype(o_ref.dtype)


def _inproj(h, w, off_gate):
    m, d = h.shape
    n = w.shape[1]
    tm = _pick((1024, 512, 256, 128), m)
    tn = _pick((512, 256, 128), n, off_gate)
    return pl.pallas_call(
        functools.partial(_inproj_kernel, gate_tile0=off_gate // tn),
        out_shape=jax.ShapeDtypeStruct((m, n), BF16),
        grid=(n // tn, m // tm),
        in_specs=[pl.BlockSpec((tm, d), lambda j, i: (i, 0)), pl.BlockSpec((d, tn), lambda j, i: (0, j))],
        out_specs=pl.BlockSpec((tm, tn), lambda j, i: (i, j)),
        scratch_shapes=[pltpu.VMEM((d, tn), BF16)],
        compiler_params=_cparams(("arbitrary", "arbitrary")),
        name="inproj",
    )(h, w)


def _conv_kernel(b_ref, c_ref, u_ref, cp_ref, up_ref, cn_ref, un_ref, w_ref, o_ref, *, seq_blocks):
    i = pl.program_id(0)
    tm = c_ref.shape[0]
    z = c_ref[...].astype(F32) * u_ref[...].astype(F32)
    last_row = BF16_SUBLANES - 1
    z_prev = cp_ref[last_row:last_row + 1, :].astype(F32) * up_ref[last_row:last_row + 1, :].astype(F32)
    z_next = cn_ref[0:1, :].astype(F32) * un_ref[0:1, :].astype(F32)
    pos = i % seq_blocks
    z_prev = jnp.where(pos == 0, 0.0, z_prev)
    z_next = jnp.where(pos == seq_blocks - 1, 0.0, z_next)
    row = lax.broadcasted_iota(jnp.int32, z.shape, 0)
    z_up = jnp.where(row == 0, z_prev, pltpu.roll(z, 1, axis=0))
    z_dn = jnp.where(row == tm - 1, z_next, pltpu.roll(z, tm - 1, axis=0))
    w = w_ref[...]
    conv = w[0:1, :] * z_up + w[1:2, :] * z + w[2:3, :] * z_dn
    o_ref[...] = (b_ref[...].astype(F32) * conv).astype(o_ref.dtype)


def _conv_mixer(p, conv_w, seq, off_b, off_c, off_u):
    m = p.shape[0]
    cw = conv_w.shape[1]
    tm = _pick((512, 256, 128), seq)
    tc = _pick((512, 256, 128), cw, off_b, off_c, off_u)
    halo = BF16_SUBLANES
    rb = tm // halo
    n_halo = m // halo

    def main(off):
        return pl.BlockSpec((tm, tc), lambda i, j: (i, off // tc + j))

    def prev(off):
        return pl.BlockSpec((halo, tc), lambda i, j: (jnp.maximum(i * rb - 1, 0), off // tc + j))

    def nxt(off):
        return pl.BlockSpec((halo, tc), lambda i, j: (jnp.minimum((i + 1) * rb, n_halo - 1), off // tc + j))

    return pl.pallas_call(
        functools.partial(_conv_kernel, seq_blocks=seq // tm),
        out_shape=jax.ShapeDtypeStruct((m, cw), BF16),
        grid=(m // tm, cw // tc),
        in_specs=[main(off_b), main(off_c), main(off_u), prev(off_c), prev(off_u), nxt(off_c), nxt(off_u),
                  pl.BlockSpec((conv_w.shape[0], tc), lambda i, j: (0, j))],
        out_specs=pl.BlockSpec((tm, tc), lambda i, j: (i, j)),
        compiler_params=_cparams(("parallel", "parallel")),
        name="conv_mixer",
    )(p, p, p, p, p, p, p, conv_w.astype(F32))


def _swa_kernel(slopes_ref, sinks_ref, q_ref, kp_ref, kc_ref, kn_ref, vp_ref, vc_ref, vn_ref, o_ref,
                *, n_kv, group, hd, nb):
    n = pl.program_id(1)
    has_prev = n > 0
    has_next = n < nb - 1
    shape = (BLOCK, 3 * BLOCK)
    krel = lax.broadcasted_iota(jnp.int32, shape, 1) - BLOCK
    dist = jnp.abs(lax.broadcasted_iota(jnp.int32, shape, 0) - krel)
    valid = (dist <= WINDOW) & ((krel >= 0) | has_prev) & ((krel < BLOCK) | has_next)
    distf = dist.astype(F32)[None]
    valid = valid[None]
    scale = 1.0 / math.sqrt(hd)
    head_in_group = lax.broadcasted_iota(jnp.int32, (group, 1, 1), 0)
    for kh in range(n_kv):
        cols = slice(kh * hd, (kh + 1) * hd)
        kband = jnp.concatenate([kp_ref[:, cols], kc_ref[:, cols], kn_ref[:, cols]], axis=0)
        vband = jnp.concatenate([vp_ref[:, cols], vc_ref[:, cols], vn_ref[:, cols]], axis=0)
        heads = [kh * group + gi for gi in range(group)]
        q = jnp.concatenate([q_ref[:, h * hd:(h + 1) * hd] for h in heads], axis=0)
        slope = jnp.zeros((group, 1, 1), F32)
        sink = jnp.zeros((group, 1, 1), F32)
        for gi, h in enumerate(heads):
            slope = jnp.where(head_in_group == gi, slopes_ref[h], slope)
            sink = jnp.where(head_in_group == gi, sinks_ref[h], sink)
        s = lax.dot_general(q, kband, (((1,), (1,)), ((), ())), preferred_element_type=F32) * scale
        s = s.reshape(group, BLOCK, 3 * BLOCK)
        s = jnp.where(valid, s - slope * distf, NEG_INF)
        mx = jnp.maximum(jnp.max(s, axis=-1, keepdims=True), sink)
        pr = jnp.exp(s - mx)
        denom = jnp.sum(pr, axis=-1, keepdims=True) + jnp.exp(sink - mx)
        o = jnp.dot(pr.reshape(group * BLOCK, 3 * BLOCK).astype(BF16), vband, preferred_element_type=F32)
        o = o.reshape(group, BLOCK, hd) / denom
        for gi, h in enumerate(heads):
            o_ref[:, h * hd:(h + 1) * hd] = o[gi].astype(o_ref.dtype)


def _windowed_attention(p, slopes, sinks, bsz, seq, off_q, off_k, off_v, aw, kvw, n_heads):
    m = p.shape[0]
    hd = aw // n_heads
    n_kv = kvw // hd
    nb = seq // BLOCK
    assert off_q % aw == 0 and off_k % kvw == 0 and off_v % kvw == 0 and seq % BLOCK == 0

    def kv_spec(off, shift):
        def imap(b, n, *_):
            return (b * nb + jnp.clip(n + shift, 0, nb - 1), off // kvw)
        return pl.BlockSpec((BLOCK, kvw), imap)

    grid_spec = pltpu.PrefetchScalarGridSpec(
        num_scalar_prefetch=2,
        grid=(bsz, nb),
        in_specs=[pl.BlockSpec((BLOCK, aw), lambda b, n, *_: (b * nb + n, off_q // aw)),
                  kv_spec(off_k, -1), kv_spec(off_k, 0), kv_spec(off_k, 1),
                  kv_spec(off_v, -1), kv_spec(off_v, 0), kv_spec(off_v, 1)],
        out_specs=pl.BlockSpec((BLOCK, aw), lambda b, n, *_: (b * nb + n, 0)),
    )
    return pl.pallas_call(
        functools.partial(_swa_kernel, n_kv=n_kv, group=n_heads // n_kv, hd=hd, nb=nb),
        out_shape=jax.ShapeDtypeStruct((m, aw), BF16),
        grid_spec=grid_spec,
        compiler_params=_cparams(("parallel", "parallel")),
        name="windowed_gqa",
    )(slopes, sinks, p, p, p, p, p, p, p)


def _merge_kernel(za_ref, ob_ref, wc_ref, wa_ref, ga_ref, gb_ref, o_ref, wcb_ref, wab_ref):
    @pl.when(pl.program_id(1) == 0)
    def _():
        wcb_ref[...] = wc_ref[...].astype(BF16)
        wab_ref[...] = wa_ref[...].astype(BF16)

    ya = jnp.dot(za_ref[...], wcb_ref[...], preferred_element_type=F32)
    yb = jnp.dot(ob_ref[...], wab_ref[...], preferred_element_type=F32)
    o_ref[...] = (ga_ref[...].astype(F32) * ya + gb_ref[...].astype(F32) * yb).astype(o_ref.dtype)


def _merge(za, ob, wc, wa, p, off_ga, off_gb):
    m, ka = za.shape
    kb = ob.shape[1]
    d = wc.shape[1]
    tm = _pick((1024, 512, 256, 128), m)
    tn = _pick((512, 256, 128), d, off_ga, off_gb)
    return pl.pallas_call(
        _merge_kernel,
        out_shape=jax.ShapeDtypeStruct((m, d), BF16),
        grid=(d // tn, m // tm),
        in_specs=[pl.BlockSpec((tm, ka), lambda j, i: (i, 0)), pl.BlockSpec((tm, kb), lambda j, i: (i, 0)),
                  pl.BlockSpec((ka, tn), lambda j, i: (0, j)), pl.BlockSpec((kb, tn), lambda j, i: (0, j)),
                  pl.BlockSpec((tm, tn), lambda j, i: (i, off_ga // tn + j)),
                  pl.BlockSpec((tm, tn), lambda j, i: (i, off_gb // tn + j))],
        out_specs=pl.BlockSpec((tm, tn), lambda j, i: (i, j)),
        scratch_shapes=[pltpu.VMEM((ka, tn), BF16), pltpu.VMEM((kb, tn), BF16)],
        compiler_params=_cparams(("arbitrary", "arbitrary")),
        name="gated_merge",
    )(za, ob, wc, wa, p, p)


def _resid_matmul_kernel(a_ref, w_ref, x_ref, o_ref, wb_ref):
    @pl.when(pl.program_id(1) == 0)
    def _():
        wb_ref[...] = w_ref[...].astype(BF16)

    o_ref[...] = x_ref[...] + jnp.dot(a_ref[...], wb_ref[...], preferred_element_type=F32)


def _resid_matmul(a, w, x):
    m, k = a.shape
    n = w.shape[1]
    tm = _pick((1024, 512, 256, 128), m)
    tn = _pick((512, 256, 128), n)
    return pl.pallas_call(
        _resid_matmul_kernel,
        out_shape=jax.ShapeDtypeStruct((m, n), F32),
        grid=(n // tn, m // tm),
        in_specs=[pl.BlockSpec((tm, k), lambda j, i: (i, 0)), pl.BlockSpec((k, tn), lambda j, i: (0, j)),
                  pl.BlockSpec((tm, tn), lambda j, i: (i, j))],
        out_specs=pl.BlockSpec((tm, tn), lambda j, i: (i, j)),
        scratch_shapes=[pltpu.VMEM((k, tn), BF16)],
        compiler_params=_cparams(("arbitrary", "arbitrary")),
        name="out_proj_residual",
    )(a, w, x)


def _mem_kv_kernel(mem_ref, g_ref, w_ref, o_ref):
    mn = _rms(mem_ref[...].astype(F32), g_ref[...]).astype(BF16)
    o_ref[...] = jnp.dot(mn, w_ref[...], preferred_element_type=F32).astype(o_ref.dtype)


def _mem_kv(mem2d, g, w):
    m, d = mem2d.shape
    n = w.shape[1]
    tn = _pick((512, 256, 128), n)
    return pl.pallas_call(
        _mem_kv_kernel,
        out_shape=jax.ShapeDtypeStruct((m, n), BF16),
        grid=(n // tn,),
        in_specs=[pl.BlockSpec((m, d), lambda j: (0, 0)), pl.BlockSpec((1, d), lambda j: (0, 0)),
                  pl.BlockSpec((d, tn), lambda j: (0, j))],
        out_specs=pl.BlockSpec((m, tn), lambda j: (0, j)),
        compiler_params=_cparams(("parallel",)),
        name="mem_kv",
    )(mem2d, g.reshape(1, d).astype(F32), w)


def _pack_halves(hb):
    half = hb.shape[1] // 2
    lo = lax.bitcast_convert_type(hb[:, :half].astype(F32), jnp.uint32)
    hi = lax.bitcast_convert_type(hb[:, half:].astype(F32), jnp.uint32)
    return (lo >> 16) | (hi & jnp.uint32(0xFFFF0000))


def _unpack_halves(w):
    lo = lax.bitcast_convert_type(w << 16, F32).astype(BF16)
    hi = lax.bitcast_convert_type(w & jnp.uint32(0xFFFF0000), F32).astype(BF16)
    return lo, hi


def _cross_kernel(x_ref, gc_ref, wq_ref, kv_ref, wo_ref, gm_ref, wrc_ref, x2_ref, hm_ref, lg_ref, *, cw):
    x = x_ref[...]
    hc = _rms(x, gc_ref[...]).astype(BF16)
    q = jnp.dot(hc, wq_ref[...], preferred_element_type=F32).astype(BF16)
    hd = cw // N_CROSS_HEADS
    inv = 1.0 / math.sqrt(hd)
    outs = []
    for h in range(N_CROSS_HEADS):
        k = kv_ref[0, :, h * hd:(h + 1) * hd]
        v = kv_ref[0, :, cw + h * hd:cw + (h + 1) * hd]
        s = lax.dot_general(q[:, h * hd:(h + 1) * hd], k, (((1,), (1,)), ((), ())),
                            preferred_element_type=F32) * inv
        pr = jnp.exp(s - jnp.max(s, axis=-1, keepdims=True))
        o = jnp.dot(pr.astype(BF16), v, preferred_element_type=F32) / jnp.sum(pr, axis=-1, keepdims=True)
        outs.append(o.astype(BF16))
    o = jnp.concatenate(outs, axis=1)
    x2 = x + jnp.dot(o, wo_ref[...], preferred_element_type=F32)
    x2_ref[...] = x2
    hm = _rms(x2, gm_ref[...])
    hi = hm.astype(BF16)
    lo = (hm - hi.astype(F32)).astype(BF16)
    hm_ref[...] = _pack_halves(hi)
    ne = lg_ref.shape[1]
    both = jnp.dot(hi, wrc_ref[...], preferred_element_type=F32)
    lg_ref[...] = both[:, :ne] + both[:, ne:] + jnp.dot(lo, wrc_ref[:, :ne], preferred_element_type=F32)


def _cross_attention(x1, g_cross, wq, kv, wo, g_moe, wr_cat, seq):
    m, d = x1.shape
    cw = wq.shape[1]
    ne = wr_cat.shape[1] // 2
    mem_len = kv.shape[1]
    tm = _pick((256, 128), seq)
    per_seq = seq // tm
    const = lambda i: (0, 0)
    return pl.pallas_call(
        functools.partial(_cross_kernel, cw=cw),
        out_shape=(jax.ShapeDtypeStruct((m, d), F32), jax.ShapeDtypeStruct((m, d // 2), jnp.uint32),
                   jax.ShapeDtypeStruct((m, ne), F32)),
        grid=(m // tm,),
        in_specs=[pl.BlockSpec((tm, d), lambda i: (i, 0)), pl.BlockSpec((1, d), const),
                  pl.BlockSpec((d, cw), const),
                  pl.BlockSpec((1, mem_len, 2 * cw), lambda i: (i // per_seq, 0, 0)),
                  pl.BlockSpec((cw, d), const), pl.BlockSpec((1, d), const),
                  pl.BlockSpec((d, 2 * ne), const)],
        out_specs=(pl.BlockSpec((tm, d), lambda i: (i, 0)), pl.BlockSpec((tm, d // 2), lambda i: (i, 0)),
                   pl.BlockSpec((tm, ne), lambda i: (i, 0))),
        compiler_params=_cparams(("parallel",)),
        name="cross_attention",
    )(x1, g_cross.reshape(1, d).astype(F32), wq, kv, wo, g_moe.reshape(1, d).astype(F32), wr_cat)


def _route_kernel(lg_ref, idx_ref, pos_ref, aff_ref, ends_ref, gt_ref, eq_ref, cgt_ref, ceq_ref, csel_ref,
                  *, cap, chunk):
    logits = lg_ref[0]
    t_len, ne = logits.shape
    ex = jnp.exp(logits - jnp.max(logits, axis=-1, keepdims=True))
    aff = ex / jnp.sum(ex, axis=-1, keepdims=True)
    aff_ref[0] = aff
    keys = lax.bitcast_convert_type(aff, jnp.int32)

    def search(it, thr):
        cand = thr | jnp.left_shift(jnp.int32(1), 30 - it)
        cnt = jnp.sum((keys >= cand).astype(jnp.int32), axis=0, keepdims=True)
        return jnp.where(cnt >= cap, cand, thr)

    thr = lax.fori_loop(0, 31, search, jnp.zeros((1, ne), jnp.int32))
    gt = keys > thr
    eq = keys == thr
    ties_taken = (cap - jnp.sum(gt.astype(jnp.int32), axis=0, keepdims=True)).astype(F32)
    gt_ref[...] = jnp.where(gt, 1.0, 0.0).astype(BF16)
    eq_ref[...] = jnp.where(eq, 1.0, 0.0).astype(BF16)

    tri = (lax.broadcasted_iota(jnp.int32, (chunk, chunk), 0)
           >= lax.broadcasted_iota(jnp.int32, (chunk, chunk), 1)).astype(BF16)

    def prefix(ci, carry):
        rows = pl.ds(pl.multiple_of(ci * chunk, chunk), chunk)
        cg = jnp.dot(tri, gt_ref[rows, :], preferred_element_type=F32) + carry[0]
        ce = jnp.dot(tri, eq_ref[rows, :], preferred_element_type=F32) + carry[1]
        cgt_ref[rows, :] = cg
        ceq_ref[rows, :] = ce
        return cg[chunk - 1:chunk, :], ce[chunk - 1:chunk, :]

    zero = jnp.zeros((1, ne), F32)
    lax.fori_loop(0, t_len // chunk, prefix, (zero, zero))
    ceq = ceq_ref[...]
    sel = gt | (eq & (ceq <= ties_taken))
    csel = (cgt_ref[...] + jnp.minimum(ceq, ties_taken)).astype(jnp.int32)
    csel_ref[...] = csel
    pos_ref[0] = jnp.where(sel, csel - 1, -1)

    slot = lax.broadcasted_iota(jnp.int32, (1, cap), 1)
    for e in range(ne):
        def count(ci, acc):
            rows = pl.ds(pl.multiple_of(ci * chunk, chunk), chunk)
            le = (csel_ref[rows, e:e + 1] <= slot).astype(jnp.int32)
            return acc + jnp.sum(le.reshape(chunk // 8, 8, cap), axis=0)
        acc = lax.fori_loop(0, t_len // chunk, count, jnp.zeros((8, cap), jnp.int32))
        idx_ref[0, e:e + 1, :] = jnp.sum(acc, axis=0, keepdims=True)
    ends_ref[0] = csel_ref[pl.ds(BLOCK - 1, t_len // BLOCK, stride=BLOCK), :]


def _route(logits, cap):
    bsz, t_len, ne = logits.shape
    chunk = _pick((256, 128), t_len)
    nb = t_len // BLOCK
    return pl.pallas_call(
        functools.partial(_route_kernel, cap=cap, chunk=chunk),
        out_shape=(jax.ShapeDtypeStruct((bsz, ne, cap), jnp.int32),
                   jax.ShapeDtypeStruct((bsz, t_len, ne), jnp.int32),
                   jax.ShapeDtypeStruct((bsz, t_len, ne), F32),
                   jax.ShapeDtypeStruct((bsz, nb, ne), jnp.int32)),
        grid=(bsz,),
        in_specs=[pl.BlockSpec((1, t_len, ne), lambda b: (b, 0, 0))],
        out_specs=(pl.BlockSpec((1, ne, cap), lambda b: (b, 0, 0)),
                   pl.BlockSpec((1, t_len, ne), lambda b: (b, 0, 0)),
                   pl.BlockSpec((1, t_len, ne), lambda b: (b, 0, 0)),
                   pl.BlockSpec((1, nb, ne), lambda b: (b, 0, 0))),
        scratch_shapes=[pltpu.VMEM((t_len, ne), BF16), pltpu.VMEM((t_len, ne), BF16),
                        pltpu.VMEM((t_len, ne), F32), pltpu.VMEM((t_len, ne), F32),
                        pltpu.VMEM((t_len, ne), jnp.int32)],
        compiler_params=_cparams(("parallel",)),
        name="expert_choice_route",
    )(logits)


def _expert_kernel(rows_ref, hm_hbm, wg_ref, wu_ref, wd_ref, y_ref, xg_ref, xb_ref, hmid_ref, sem,
                   *, cap, n_f, n_d, tf, n_groups):
    j = pl.program_id(2)
    group = pl.program_id(0) * pl.num_programs(1) + pl.program_id(1)
    half = xg_ref.shape[1]
    rows_per_step = cap // n_d

    def row_copy(g, r):
        return pltpu.make_async_copy(hm_hbm.at[pl.ds(rows_ref[g * cap + r], 1), :],
                                     xg_ref.at[pl.ds(r, 1), :], sem)

    def wait_gather(g):
        def body(r, c):
            row_copy(g, r).wait()
            return c
        lax.fori_loop(0, cap, body, 0, unroll=8)

    @pl.when(j == 0)
    def _():
        @pl.when(group == 0)
        def _():
            def body(r, c):
                row_copy(group, r).start()
                return c
            lax.fori_loop(0, cap, body, 0, unroll=8)

        wait_gather(group)
        lo, hi = _unpack_halves(xg_ref[...])
        xb_ref[:, :half] = lo
        xb_ref[:, half:] = hi

    @pl.when(j < n_f)
    def _():
        x = xb_ref[...]
        a = jnp.dot(x, wg_ref[0].astype(BF16), preferred_element_type=F32)
        u = jnp.dot(x, wu_ref[0].astype(BF16), preferred_element_type=F32)
        hval = (a * _sigmoid(a) * u).astype(BF16)
        for jj in range(n_f):
            @pl.when(j == jj)
            def _():
                hmid_ref[:, jj * tf:(jj + 1) * tf] = hval

    @pl.when(j >= n_f)
    def _():
        nxt = jnp.minimum(group + 1, n_groups - 1)
        base = (j - n_f) * rows_per_step
        for r in range(rows_per_step):
            row_copy(nxt, base + r).start()
        y = jnp.dot(hmid_ref[...], wd_ref[0].astype(BF16), preferred_element_type=F32)
        y_ref[0, 0, :cap, :] = y.astype(y_ref.dtype)
        y_ref[0, 0, cap:, :] = jnp.zeros((y_ref.shape[2] - cap, y_ref.shape[3]), y_ref.dtype)

    @pl.when((group == n_groups - 1) & (j == n_f + n_d - 1))
    def _():
        wait_gather(group)


def _experts(rows, hm_packed, wg, wu, wd, bsz, cap):
    ne, d, ff = wg.shape
    tf = _pick((256, 128), ff)
    td = _pick((512, 256, 128), d)
    n_f = ff // tf
    n_d = d // td
    cp = cap + COMBINE_WIN
    grid_spec = pltpu.PrefetchScalarGridSpec(
        num_scalar_prefetch=1,
        grid=(ne, bsz, n_f + n_d),
        in_specs=[pl.BlockSpec(memory_space=pl.ANY),
                  pl.BlockSpec((1, d, tf), lambda e, b, j, *_: (e, 0, jnp.minimum(j, n_f - 1))),
                  pl.BlockSpec((1, d, tf), lambda e, b, j, *_: (e, 0, jnp.minimum(j, n_f - 1))),
                  pl.BlockSpec((1, ff, td), lambda e, b, j, *_: (e, 0, jnp.maximum(j - n_f, 0)))],
        out_specs=pl.BlockSpec((1, 1, cp, td), lambda e, b, j, *_: (e, b, 0, jnp.maximum(j - n_f, 0))),
        scratch_shapes=[pltpu.VMEM((cap, d // 2), jnp.uint32), pltpu.VMEM((cap, d), BF16),
                        pltpu.VMEM((cap, ff), BF16), pltpu.SemaphoreType.DMA(())],
    )
    return pl.pallas_call(
        functools.partial(_expert_kernel, cap=cap, n_f=n_f, n_d=n_d, tf=tf, n_groups=ne * bsz),
        out_shape=jax.ShapeDtypeStruct((ne, bsz, cp, d), BF16),
        grid_spec=grid_spec,
        compiler_params=_cparams(("arbitrary", "arbitrary", "arbitrary")),
        name="expert_ffn",
    )(rows, hm_packed, wg, wu, wd)


def _combine_kernel(start_ref, small_ref, x2_ref, pos_ref, aff_ref, g_ref, y_hbm, o_ref,
                    fast_ref, slow_ref, fast_sem, slow_sem, *, ne, nb, bsz):
    b = pl.program_id(0)
    j = pl.program_id(1)
    step = b * nb + j
    total = bsz * nb
    d = fast_ref.shape[3]

    def window_copies(st, win, dst, sem):
        bb = st // nb
        jj = st - bb * nb
        return [pltpu.make_async_copy(
            y_hbm.at[e, bb, pl.ds(pl.multiple_of(start_ref[(bb * ne + e) * nb + jj], BF16_SUBLANES), win), :],
            dst.at[e], sem) for e in range(ne)]

    def small_copies(st):
        slot = st % 2
        return window_copies(st, COMBINE_WIN_SMALL, fast_ref.at[slot], fast_sem.at[slot])

    @pl.when((step == 0) & (small_ref[0] == 1))
    def _():
        for c in small_copies(step):
            c.start()

    nxt = jnp.minimum(step + 1, total - 1)

    @pl.when((step + 1 < total) & (small_ref[nxt] == 1))
    def _():
        for c in small_copies(nxt):
            c.start()

    pos = pos_ref[0]
    gate = aff_ref[0]

    def finish(win, wins):
        lane = lax.broadcasted_iota(jnp.int32, (1, win), 1)
        parts = []
        for e in range(ne):
            rel = pos[:, e:e + 1] - start_ref[(b * ne + e) * nb + j]
            parts.append(jnp.where(rel == lane, gate[:, e:e + 1], 0.0).astype(BF16))
        sel = jnp.concatenate(parts, axis=1)
        x3 = x2_ref[...] + jnp.dot(sel, wins.reshape(ne * win, d), preferred_element_type=F32)
        o_ref[...] = _rms(x3, g_ref[...])

    @pl.when(small_ref[step] == 1)
    def _():
        for c in small_copies(step):
            c.wait()
        finish(COMBINE_WIN_SMALL, fast_ref[step % 2])

    @pl.when(small_ref[step] != 1)
    def _():
        copies = window_copies(step, COMBINE_WIN, slow_ref, slow_sem)
        for c in copies:
            c.start()
        for c in copies:
            c.wait()
        finish(COMBINE_WIN, slow_ref[...])


def _combine(starts, small, x2, pos, aff, g_final, y, seq):
    m, d = x2.shape
    ne, bsz = y.shape[0], y.shape[1]
    nb = seq // BLOCK
    grid_spec = pltpu.PrefetchScalarGridSpec(
        num_scalar_prefetch=2,
        grid=(bsz, nb),
        in_specs=[pl.BlockSpec((BLOCK, d), lambda b, j, *_: (b * nb + j, 0)),
                  pl.BlockSpec((1, BLOCK, ne), lambda b, j, *_: (b, j, 0)),
                  pl.BlockSpec((1, BLOCK, ne), lambda b, j, *_: (b, j, 0)),
                  pl.BlockSpec((1, d), lambda b, j, *_: (0, 0)),
                  pl.BlockSpec(memory_space=pl.ANY)],
        out_specs=pl.BlockSpec((BLOCK, d), lambda b, j, *_: (b * nb + j, 0)),
        scratch_shapes=[pltpu.VMEM((2, ne, COMBINE_WIN_SMALL, d), BF16), pltpu.VMEM((ne, COMBINE_WIN, d), BF16),
                        pltpu.SemaphoreType.DMA((2,)), pltpu.SemaphoreType.DMA(())],
    )
    return pl.pallas_call(
        functools.partial(_combine_kernel, ne=ne, nb=nb, bsz=bsz),
        out_shape=jax.ShapeDtypeStruct((m, d), F32),
        grid_spec=grid_spec,
        compiler_params=_cparams(("arbitrary", "arbitrary")),
        name="moe_combine_final_norm",
    )(starts, small, x2, pos, aff, g_final.reshape(1, d).astype(F32), y)


def kernel(x, mem, g_mix, w_in, conv_w, attn_sinks, w_conv_out, w_attn_out, w_out, g_cross, g_mem, w_q_cross,
           w_kv_cross, w_o_cross, g_moe, w_router, w_gate_e, w_up_e, w_down_e, g_final):
    bsz, seq, d = x.shape
    m = bsz * seq
    cw = conv_w.shape[1]
    aw = w_attn_out.shape[0]
    n_heads = attn_sinks.shape[0]
    in_cols = w_in.shape[1]
    kvw = (in_cols - 3 * cw - aw - 2 * d) // 2
    off_b, off_c, off_u = 0, cw, 2 * cw
    off_q = 3 * cw
    off_k = off_q + aw
    off_v = off_k + kvw
    off_ga = off_v + kvw
    off_gb = off_ga + d
    ne = w_router.shape[1]
    cap = CAPACITY_FACTOR * seq // ne

    x2d = x.reshape(m, d)

    h = _rmsnorm(x2d, g_mix, BF16)
    p = _inproj(h, w_in, off_ga)
    za = _conv_mixer(p, conv_w, seq, off_b, off_c, off_u)
    slopes = jnp.power(2.0, -8.0 * (jnp.arange(n_heads, dtype=F32) + 1.0) / n_heads)
    ob = _windowed_attention(p, slopes, attn_sinks.astype(F32), bsz, seq, off_q, off_k, off_v, aw, kvw, n_heads)
    mix = _merge(za, ob, w_conv_out, w_attn_out, p, off_ga, off_gb)
    x1 = _resid_matmul(mix, w_out, x2d)

    mem_len = mem.shape[1]
    kv = _mem_kv(mem.reshape(bsz * mem_len, d), g_mem, w_kv_cross.astype(BF16)).reshape(bsz, mem_len, -1)
    wr_hi = w_router.astype(BF16)
    wr_lo = (w_router - wr_hi.astype(F32)).astype(BF16)
    x2, hm_packed, logits = _cross_attention(x1, g_cross, w_q_cross.astype(BF16), kv, w_o_cross.astype(BF16),
                                             g_moe, jnp.concatenate([wr_hi, wr_lo], axis=1), seq)

    idx, pos, aff, ends = _route(logits.reshape(bsz, seq, ne), cap)
    rows = (idx + (jnp.arange(bsz, dtype=jnp.int32) * seq)[:, None, None]).transpose(1, 0, 2).reshape(-1)
    y = _experts(rows, hm_packed, w_gate_e, w_up_e, w_down_e, bsz, cap)
    first = jnp.concatenate([jnp.zeros((bsz, 1, ne), jnp.int32), ends[:, :-1, :]], axis=1)
    starts = (first // BF16_SUBLANES) * BF16_SUBLANES
    small = jnp.all(ends - starts <= COMBINE_WIN_SMALL, axis=2).astype(jnp.int32).reshape(-1)
    out = _combine(starts.transpose(0, 2, 1).reshape(-1), small, x2, pos, aff, g_final, y, seq)
    return out.reshape(bsz, seq, d)
```

```python
import functools
import math

import jax
import jax.numpy as jnp
from jax import lax
from jax.experimental import pallas as pl
from jax.experimental.pallas import tpu as pltpu

RMS_EPS = 1e-6
NEG_INF = -1e30
WINDOW = 128
BLOCK = 128
ROUTE_CHUNK = 256
N_CROSS_HEADS = 4
CAPACITY_FACTOR = 2
BF16_SUBLANES = 16
COMBINE_WIN = BLOCK + BF16_SUBLANES
COMBINE_WIN_SMALL = 3 * BF16_SUBLANES
VMEM_LIMIT_BYTES = 56 * 1024 * 1024

F32 = jnp.float32
BF16 = jnp.bfloat16


def _cparams(semantics, vmem=VMEM_LIMIT_BYTES):
    return pltpu.CompilerParams(dimension_semantics=semantics, vmem_limit_bytes=vmem)


def _pick(prefs, *sizes):
    for t in prefs:
        if all(s % t == 0 for s in sizes):
            return t
    raise ValueError(f"no tile in {prefs} divides {sizes}")


def _rms(x, g):
    return x * lax.rsqrt(jnp.mean(x * x, axis=-1, keepdims=True) + RMS_EPS) * g


def _sigmoid(x):
    return 1.0 / (1.0 + jnp.exp(-x))


def _rmsnorm_kernel(x_ref, g_ref, o_ref):
    o_ref[...] = _rms(x_ref[...].astype(F32), g_ref[...]).astype(o_ref.dtype)


def _rmsnorm(x2d, g, out_dtype):
    m, d = x2d.shape
    tm = _pick((256, 128, 64, 8), m)
    return pl.pallas_call(
        _rmsnorm_kernel,
        out_shape=jax.ShapeDtypeStruct((m, d), out_dtype),
        grid=(m // tm,),
        in_specs=[pl.BlockSpec((tm, d), lambda i: (i, 0)), pl.BlockSpec((1, d), lambda i: (0, 0))],
        out_specs=pl.BlockSpec((tm, d), lambda i: (i, 0)),
        compiler_params=_cparams(("parallel",)),
        name="rmsnorm",
    )(x2d, g.reshape(1, d).astype(F32))


def _inproj_kernel(h_ref, w_hbm, o_ref, stage_ref, wb_ref, sem, *, gate_tile0, tn, n_tiles):
    j = pl.program_id(0)

    def tile_copy(jj):
        return pltpu.make_async_copy(w_hbm.at[:, pl.ds(pl.multiple_of(jj * tn, tn), tn)], stage_ref, sem)

    @pl.when(pl.program_id(1) == 0)
    def _():
        @pl.when(j == 0)
        def _():
            tile_copy(j).start()

        tile_copy(j).wait()
        wb_ref[...] = stage_ref[...].astype(BF16)

        @pl.when(j + 1 < n_tiles)
        def _():
            tile_copy(j + 1).start()

    acc = jnp.dot(h_ref[...], wb_ref[...], preferred_element_type=F32)

    @pl.when(j < gate_tile0)
    def _():
        o_ref[...] = acc.astype(o_ref.dtype)

    @pl.when(j >= gate_tile0)
    def _():
        o_ref[...] = _sigmoid(acc).astype(o_ref.dtype)


def _inproj(h, w, off_gate):
    m, d = h.shape
    n = w.shape[1]
    tm = _pick((1024, 512, 256, 128), m)
    tn = _pick((1024, 512, 256, 128), n, off_gate)
    return pl.pallas_call(
        functools.partial(_inproj_kernel, gate_tile0=off_gate // tn, tn=tn, n_tiles=n // tn),
        out_shape=jax.ShapeDtypeStruct((m, n), BF16),
        grid=(n // tn, m // tm),
        in_specs=[pl.BlockSpec((tm, d), lambda j, i: (i, 0)), pl.BlockSpec(memory_space=pl.ANY)],
        out_specs=pl.BlockSpec((tm, tn), lambda j, i: (i, j)),
        scratch_shapes=[pltpu.VMEM((d, tn), F32), pltpu.VMEM((d, tn), BF16), pltpu.SemaphoreType.DMA(())],
        compiler_params=_cparams(("arbitrary", "arbitrary")),
        name="inproj",
    )(h, w)


def _conv_kernel(b_ref, c_ref, u_ref, cp_ref, up_ref, cn_ref, un_ref, w_ref, o_ref, *, seq_blocks):
    i = pl.program_id(0)
    tm = c_ref.shape[0]
    z = c_ref[...].astype(F32) * u_ref[...].astype(F32)
    last_row = BF16_SUBLANES - 1
    z_prev = cp_ref[last_row:last_row + 1, :].astype(F32) * up_ref[last_row:last_row + 1, :].astype(F32)
    z_next = cn_ref[0:1, :].astype(F32) * un_ref[0:1, :].astype(F32)
    pos = i % seq_blocks
    z_prev = jnp.where(pos == 0, 0.0, z_prev)
    z_next = jnp.where(pos == seq_blocks - 1, 0.0, z_next)
    row = lax.broadcasted_iota(jnp.int32, z.shape, 0)
    z_up = jnp.where(row == 0, z_prev, pltpu.roll(z, 1, axis=0))
    z_dn = jnp.where(row == tm - 1, z_next, pltpu.roll(z, tm - 1, axis=0))
    w = w_ref[...]
    conv = w[0:1, :] * z_up + w[1:2, :] * z + w[2:3, :] * z_dn
    o_ref[...] = (b_ref[...].astype(F32) * conv).astype(o_ref.dtype)


def _conv_mixer(p, conv_w, seq, off_b, off_c, off_u):
    m = p.shape[0]
    cw = conv_w.shape[1]
    tm = _pick((512, 256, 128), seq)
    tc = _pick((512, 256, 128), cw, off_b, off_c, off_u)
    halo = BF16_SUBLANES
    rb = tm // halo
    n_halo = m // halo

    def main(off):
        return pl.BlockSpec((tm, tc), lambda i, j: (i, off // tc + j))

    def prev(off):
        return pl.BlockSpec((halo, tc), lambda i, j: (jnp.maximum(i * rb - 1, 0), off // tc + j))

    def nxt(off):
        return pl.BlockSpec((halo, tc), lambda i, j: (jnp.minimum((i + 1) * rb, n_halo - 1), off // tc + j))

    return pl.pallas_call(
        functools.partial(_conv_kernel, seq_blocks=seq // tm),
        out_shape=jax.ShapeDtypeStruct((m, cw), BF16),
        grid=(m // tm, cw // tc),
        in_specs=[main(off_b), main(off_c), main(off_u), prev(off_c), prev(off_u), nxt(off_c), nxt(off_u),
                  pl.BlockSpec((conv_w.shape[0], tc), lambda i, j: (0, j))],
        out_specs=pl.BlockSpec((tm, tc), lambda i, j: (i, j)),
        compiler_params=_cparams(("parallel", "parallel")),
        name="conv_mixer",
    )(p, p, p, p, p, p, p, conv_w.astype(F32))


def _swa_kernel(slopes_ref, sinks_ref, q_ref, kp_ref, kc_ref, kn_ref, vp_ref, vc_ref, vn_ref, o_ref,
                *, n_kv, group, hd, nb):
    n = pl.program_id(1)
    has_prev = n > 0
    has_next = n < nb - 1
    shape = (BLOCK, 3 * BLOCK)
    krel = lax.broadcasted_iota(jnp.int32, shape, 1) - BLOCK
    dist = jnp.abs(lax.broadcasted_iota(jnp.int32, shape, 0) - krel)
    valid = (dist <= WINDOW) & ((krel >= 0) | has_prev) & ((krel < BLOCK) | has_next)
    distf = dist.astype(F32)[None]
    valid = valid[None]
    scale = 1.0 / math.sqrt(hd)
    head_in_group = lax.broadcasted_iota(jnp.int32, (group, 1, 1), 0)
    for kh in range(n_kv):
        cols = slice(kh * hd, (kh + 1) * hd)
        kband = jnp.concatenate([kp_ref[:, cols], kc_ref[:, cols], kn_ref[:, cols]], axis=0)
        vband = jnp.concatenate([vp_ref[:, cols], vc_ref[:, cols], vn_ref[:, cols]], axis=0)
        heads = [kh * group + gi for gi in range(group)]
        q = jnp.concatenate([q_ref[:, h * hd:(h + 1) * hd] for h in heads], axis=0)
        slope = jnp.zeros((group, 1, 1), F32)
        sink = jnp.zeros((group, 1, 1), F32)
        for gi, h in enumerate(heads):
            slope = jnp.where(head_in_group == gi, slopes_ref[h], slope)
            sink = jnp.where(head_in_group == gi, sinks_ref[h], sink)
        s = lax.dot_general(q, kband, (((1,), (1,)), ((), ())), preferred_element_type=F32) * scale
        s = s.reshape(group, BLOCK, 3 * BLOCK)
        s = jnp.where(valid, s - slope * distf, NEG_INF)
        mx = jnp.maximum(jnp.max(s, axis=-1, keepdims=True), sink)
        pr = jnp.exp(s - mx)
        denom = jnp.sum(pr, axis=-1, keepdims=True) + jnp.exp(sink - mx)
        o = jnp.dot(pr.reshape(group * BLOCK, 3 * BLOCK).astype(BF16), vband, preferred_element_type=F32)
        o = o.reshape(group, BLOCK, hd) / denom
        for gi, h in enumerate(heads):
            o_ref[:, h * hd:(h + 1) * hd] = o[gi].astype(o_ref.dtype)


def _windowed_attention(p, slopes, sinks, bsz, seq, off_q, off_k, off_v, aw, kvw, n_heads):
    m = p.shape[0]
    hd = aw // n_heads
    n_kv = kvw // hd
    nb = seq // BLOCK
    assert off_q % aw == 0 and off_k % kvw == 0 and off_v % kvw == 0 and seq % BLOCK == 0

    def kv_spec(off, shift):
        def imap(b, n, *_):
            return (b * nb + jnp.clip(n + shift, 0, nb - 1), off // kvw)
        return pl.BlockSpec((BLOCK, kvw), imap)

    grid_spec = pltpu.PrefetchScalarGridSpec(
        num_scalar_prefetch=2,
        grid=(bsz, nb),
        in_specs=[pl.BlockSpec((BLOCK, aw), lambda b, n, *_: (b * nb + n, off_q // aw)),
                  kv_spec(off_k, -1), kv_spec(off_k, 0), kv_spec(off_k, 1),
                  kv_spec(off_v, -1), kv_spec(off_v, 0), kv_spec(off_v, 1)],
        out_specs=pl.BlockSpec((BLOCK, aw), lambda b, n, *_: (b * nb + n, 0)),
    )
    return pl.pallas_call(
        functools.partial(_swa_kernel, n_kv=n_kv, group=n_heads // n_kv, hd=hd, nb=nb),
        out_shape=jax.ShapeDtypeStruct((m, aw), BF16),
        grid_spec=grid_spec,
        compiler_params=_cparams(("parallel", "parallel")),
        name="windowed_gqa",
    )(slopes, sinks, p, p, p, p, p, p, p)


def _merge_kernel(za_ref, ob_ref, wc_ref, wa_ref, ga_ref, gb_ref, o_ref, wcb_ref, wab_ref):
    @pl.when(pl.program_id(1) == 0)
    def _():
        wcb_ref[...] = wc_ref[...].astype(BF16)
        wab_ref[...] = wa_ref[...].astype(BF16)

    ya = jnp.dot(za_ref[...], wcb_ref[...], preferred_element_type=F32)
    yb = jnp.dot(ob_ref[...], wab_ref[...], preferred_element_type=F32)
    o_ref[...] = (ga_ref[...].astype(F32) * ya + gb_ref[...].astype(F32) * yb).astype(o_ref.dtype)


def _merge(za, ob, wc, wa, p, off_ga, off_gb):
    m, ka = za.shape
    kb = ob.shape[1]
    d = wc.shape[1]
    tm = _pick((1024, 512, 256, 128), m)
    tn = _pick((512, 256, 128), d, off_ga, off_gb)
    return pl.pallas_call(
        _merge_kernel,
        out_shape=jax.ShapeDtypeStruct((m, d), BF16),
        grid=(d // tn, m // tm),
        in_specs=[pl.BlockSpec((tm, ka), lambda j, i: (i, 0)), pl.BlockSpec((tm, kb), lambda j, i: (i, 0)),
                  pl.BlockSpec((ka, tn), lambda j, i: (0, j)), pl.BlockSpec((kb, tn), lambda j, i: (0, j)),
                  pl.BlockSpec((tm, tn), lambda j, i: (i, off_ga // tn + j)),
                  pl.BlockSpec((tm, tn), lambda j, i: (i, off_gb // tn + j))],
        out_specs=pl.BlockSpec((tm, tn), lambda j, i: (i, j)),
        scratch_shapes=[pltpu.VMEM((ka, tn), BF16), pltpu.VMEM((kb, tn), BF16)],
        compiler_params=_cparams(("arbitrary", "arbitrary")),
        name="gated_merge",
    )(za, ob, wc, wa, p, p)


def _resid_matmul_kernel(a_ref, w_ref, x_ref, o_ref, wb_ref):
    @pl.when(pl.program_id(1) == 0)
    def _():
        wb_ref[...] = w_ref[...].astype(BF16)

    o_ref[...] = x_ref[...] + jnp.dot(a_ref[...], wb_ref[...], preferred_element_type=F32)


def _resid_matmul(a, w, x):
    m, k = a.shape
    n = w.shape[1]
    tm = _pick((1024, 512, 256, 128), m)
    tn = _pick((512, 256, 128), n)
    return pl.pallas_call(
        _resid_matmul_kernel,
        out_shape=jax.ShapeDtypeStruct((m, n), F32),
        grid=(n // tn, m // tm),
        in_specs=[pl.BlockSpec((tm, k), lambda j, i: (i, 0)), pl.BlockSpec((k, tn), lambda j, i: (0, j)),
                  pl.BlockSpec((tm, tn), lambda j, i: (i, j))],
        out_specs=pl.BlockSpec((tm, tn), lambda j, i: (i, j)),
        scratch_shapes=[pltpu.VMEM((k, tn), BF16)],
        compiler_params=_cparams(("arbitrary", "arbitrary")),
        name="out_proj_residual",
    )(a, w, x)


def _mem_kv_kernel(mem_ref, g_ref, w_ref, o_ref):
    mn = _rms(mem_ref[...].astype(F32), g_ref[...]).astype(BF16)
    o_ref[...] = jnp.dot(mn, w_ref[...], preferred_element_type=F32).astype(o_ref.dtype)


def _mem_kv(mem2d, g, w):
    m, d = mem2d.shape
    n = w.shape[1]
    tn = _pick((512, 256, 128), n)
    return pl.pallas_call(
        _mem_kv_kernel,
        out_shape=jax.ShapeDtypeStruct((m, n), BF16),
        grid=(n // tn,),
        in_specs=[pl.BlockSpec((m, d), lambda j: (0, 0)), pl.BlockSpec((1, d), lambda j: (0, 0)),
                  pl.BlockSpec((d, tn), lambda j: (0, j))],
        out_specs=pl.BlockSpec((m, tn), lambda j: (0, j)),
        compiler_params=_cparams(("parallel",)),
        name="mem_kv",
    )(mem2d, g.reshape(1, d).astype(F32), w)


def _pack_halves(hb):
    half = hb.shape[1] // 2
    lo = lax.bitcast_convert_type(hb[:, :half].astype(F32), jnp.uint32)
    hi = lax.bitcast_convert_type(hb[:, half:].astype(F32), jnp.uint32)
    return (lo >> 16) | (hi & jnp.uint32(0xFFFF0000))


def _unpack_halves(w):
    lo = lax.bitcast_convert_type(w << 16, F32).astype(BF16)
    hi = lax.bitcast_convert_type(w & jnp.uint32(0xFFFF0000), F32).astype(BF16)
    return lo, hi


def _cross_kernel(x_ref, gc_ref, wq_ref, kv_ref, wo_ref, gm_ref, wrc_ref, x2_ref, hm_ref, lg_ref, *, cw):
    x = x_ref[...]
    hc = _rms(x, gc_ref[...]).astype(BF16)
    q = jnp.dot(hc, wq_ref[...], preferred_element_type=F32).astype(BF16)
    hd = cw // N_CROSS_HEADS
    inv = 1.0 / math.sqrt(hd)
    outs = []
    for h in range(N_CROSS_HEADS):
        k = kv_ref[0, :, h * hd:(h + 1) * hd]
        v = kv_ref[0, :, cw + h * hd:cw + (h + 1) * hd]
        s = lax.dot_general(q[:, h * hd:(h + 1) * hd], k, (((1,), (1,)), ((), ())),
                            preferred_element_type=F32) * inv
        pr = jnp.exp(s - jnp.max(s, axis=-1, keepdims=True))
        o = jnp.dot(pr.astype(BF16), v, preferred_element_type=F32) / jnp.sum(pr, axis=-1, keepdims=True)
        outs.append(o.astype(BF16))
    o = jnp.concatenate(outs, axis=1)
    x2 = x + jnp.dot(o, wo_ref[...], preferred_element_type=F32)
    x2_ref[...] = x2
    hm = _rms(x2, gm_ref[...])
    hi = hm.astype(BF16)
    lo = (hm - hi.astype(F32)).astype(BF16)
    hm_ref[...] = _pack_halves(hi)
    ne = lg_ref.shape[1]
    both = jnp.dot(hi, wrc_ref[...], preferred_element_type=F32)
    lg_ref[...] = both[:, :ne] + both[:, ne:] + jnp.dot(lo, wrc_ref[:, :ne], preferred_element_type=F32)


def _cross_attention(x1, g_cross, wq, kv, wo, g_moe, wr_cat, seq):
    m, d = x1.shape
    cw = wq.shape[1]
    ne = wr_cat.shape[1] // 2
    mem_len = kv.shape[1]
    tm = _pick((256, 128), seq)
    per_seq = seq // tm
    const = lambda i: (0, 0)
    return pl.pallas_call(
        functools.partial(_cross_kernel, cw=cw),
        out_shape=(jax.ShapeDtypeStruct((m, d), F32), jax.ShapeDtypeStruct((m, d // 2), jnp.uint32),
                   jax.ShapeDtypeStruct((m, ne), F32)),
        grid=(m // tm,),
        in_specs=[pl.BlockSpec((tm, d), lambda i: (i, 0)), pl.BlockSpec((1, d), const),
                  pl.BlockSpec((d, cw), const),
                  pl.BlockSpec((1, mem_len, 2 * cw), lambda i: (i // per_seq, 0, 0)),
                  pl.BlockSpec((cw, d), const), pl.BlockSpec((1, d), const),
                  pl.BlockSpec((d, 2 * ne), const)],
        out_specs=(pl.BlockSpec((tm, d), lambda i: (i, 0)), pl.BlockSpec((tm, d // 2), lambda i: (i, 0)),
                   pl.BlockSpec((tm, ne), lambda i: (i, 0))),
        compiler_params=_cparams(("parallel",)),
        name="cross_attention",
    )(x1, g_cross.reshape(1, d).astype(F32), wq, kv, wo, g_moe.reshape(1, d).astype(F32), wr_cat)


def _route_kernel(lg_ref, pos_ref, aff_ref, csel_ref, ends_ref, gt_ref, eq_ref, cgt_ref, ceq_ref, *, cap, chunk):
    logits = lg_ref[0]
    t_len, ne = logits.shape
    ex = jnp.exp(logits - jnp.max(logits, axis=-1, keepdims=True))
    aff = ex / jnp.sum(ex, axis=-1, keepdims=True)
    aff_ref[0] = aff
    keys = lax.bitcast_convert_type(aff, jnp.int32)

    def search(it, thr):
        cand = thr | jnp.left_shift(jnp.int32(1), 30 - it)
        cnt = jnp.sum((keys >= cand).astype(jnp.int32), axis=0, keepdims=True)
        return jnp.where(cnt >= cap, cand, thr)

    thr = lax.fori_loop(0, 31, search, jnp.zeros((1, ne), jnp.int32))
    gt = keys > thr
    eq = keys == thr
    ties_taken = (cap - jnp.sum(gt.astype(jnp.int32), axis=0, keepdims=True)).astype(F32)
    gt_ref[...] = jnp.where(gt, 1.0, 0.0).astype(BF16)
    eq_ref[...] = jnp.where(eq, 1.0, 0.0).astype(BF16)

    tri = (lax.broadcasted_iota(jnp.int32, (chunk, chunk), 0)
           >= lax.broadcasted_iota(jnp.int32, (chunk, chunk), 1)).astype(BF16)

    def prefix(ci, carry):
        rows = pl.ds(pl.multiple_of(ci * chunk, chunk), chunk)
        cg = jnp.dot(tri, gt_ref[rows, :], preferred_element_type=F32) + carry[0]
        ce = jnp.dot(tri, eq_ref[rows, :], preferred_element_type=F32) + carry[1]
        cgt_ref[rows, :] = cg
        ceq_ref[rows, :] = ce
        return cg[chunk - 1:chunk, :], ce[chunk - 1:chunk, :]

    zero = jnp.zeros((1, ne), F32)
    lax.fori_loop(0, t_len // chunk, prefix, (zero, zero))
    ceq = ceq_ref[...]
    sel = gt | (eq & (ceq <= ties_taken))
    csel = (cgt_ref[...] + jnp.minimum(ceq, ties_taken)).astype(jnp.int32)
    csel_ref[0] = csel
    pos_ref[0] = jnp.where(sel, csel - 1, -1)
    ends_ref[0] = csel_ref[0, pl.ds(BLOCK - 1, t_len // BLOCK, stride=BLOCK), :]


def _route(logits, cap):
    bsz, t_len, ne = logits.shape
    chunk = _pick((ROUTE_CHUNK, BLOCK), t_len)
    nb = t_len // BLOCK
    per_seq = lambda b: (b, 0, 0)
    return pl.pallas_call(
        functools.partial(_route_kernel, cap=cap, chunk=chunk),
        out_shape=(jax.ShapeDtypeStruct((bsz, t_len, ne), jnp.int32),
                   jax.ShapeDtypeStruct((bsz, t_len, ne), F32),
                   jax.ShapeDtypeStruct((bsz, t_len, ne), jnp.int32),
                   jax.ShapeDtypeStruct((bsz, nb, ne), jnp.int32)),
        grid=(bsz,),
        in_specs=[pl.BlockSpec((1, t_len, ne), per_seq)],
        out_specs=(pl.BlockSpec((1, t_len, ne), per_seq), pl.BlockSpec((1, t_len, ne), per_seq),
                   pl.BlockSpec((1, t_len, ne), per_seq), pl.BlockSpec((1, nb, ne), per_seq)),
        scratch_shapes=[pltpu.VMEM((t_len, ne), BF16), pltpu.VMEM((t_len, ne), BF16),
                        pltpu.VMEM((t_len, ne), F32), pltpu.VMEM((t_len, ne), F32)],
        compiler_params=_cparams(("parallel",)),
        name="expert_choice_route",
    )(logits)


def _slot_tokens_kernel(lo_ref, hi_ref, csel_ref, idx_ref, part_ref, *, cap, chunk, tw):
    b = pl.program_id(0)
    ne = csel_ref.shape[1]
    n_tiles = cap // tw
    lane_tiles = chunk // BLOCK
    ones = jnp.ones((8, BLOCK), BF16)
    for e in range(ne):
        full = []
        for ct in range(n_tiles):
            slot = lax.broadcasted_iota(jnp.int32, (tw, 1), 0) + ct * tw
            lo = lo_ref[(b * ne + e) * n_tiles + ct]
            hi = hi_ref[(b * ne + e) * n_tiles + ct]

            def count(ci, acc):
                le = jnp.where(csel_ref[0, e, pl.ds(ci, 1), :] <= slot, 1.0, 0.0)
                for lt in range(lane_tiles):
                    acc = acc + le[:, lt * BLOCK:(lt + 1) * BLOCK]
                return acc

            acc = lax.fori_loop(lo, hi, count, jnp.zeros((tw, BLOCK), F32))
            part_ref[ct * tw:(ct + 1) * tw, :] = acc.astype(BF16)
            full.append(jnp.full((1, tw), lo * chunk, jnp.int32))
        total = lax.dot_general(ones, part_ref[...], (((1,), (1,)), ((), ())), preferred_element_type=F32)
        idx_ref[0, e:e + 1, :] = total[0:1, :].astype(jnp.int32) + jnp.concatenate(full, axis=1)


def _slot_tokens(csel, cap):
    bsz, t_len, ne = csel.shape
    chunk = _pick((ROUTE_CHUNK, BLOCK), t_len)
    tw = math.gcd(cap, BLOCK)
    tile_first = jnp.arange(cap // tw, dtype=jnp.int32) * tw
    chunk_first = csel[:, 0::chunk, :, None]
    chunk_last = csel[:, chunk - 1::chunk, :, None]
    lo = jnp.sum(chunk_last <= tile_first, axis=1, dtype=jnp.int32)
    hi = jnp.sum(chunk_first <= tile_first + (tw - 1), axis=1, dtype=jnp.int32)
    n_chunks = t_len // chunk
    assert t_len // BLOCK <= 256
    csel_t = csel.transpose(0, 2, 1).reshape(bsz, ne, n_chunks, chunk)
    grid_spec = pltpu.PrefetchScalarGridSpec(
        num_scalar_prefetch=2,
        grid=(bsz,),
        in_specs=[pl.BlockSpec((1, ne, n_chunks, chunk), lambda b, *_: (b, 0, 0, 0))],
        out_specs=pl.BlockSpec((1, ne, cap), lambda b, *_: (b, 0, 0)),
        scratch_shapes=[pltpu.VMEM((cap, BLOCK), BF16)],
    )
    return pl.pallas_call(
        functools.partial(_slot_tokens_kernel, cap=cap, chunk=chunk, tw=tw),
        out_shape=jax.ShapeDtypeStruct((bsz, ne, cap), jnp.int32),
        grid_spec=grid_spec,
        compiler_params=_cparams(("parallel",)),
        name="slot_tokens",
    )(lo.reshape(-1), hi.reshape(-1), csel_t)


def _expert_kernel(rows_ref, hm_hbm, wg_ref, wu_ref, wd_ref, y_ref, xg_ref, xb_ref, hmid_ref, sem,
                   *, cap, n_f, n_d, tf, n_groups):
    j = pl.program_id(2)
    group = pl.program_id(0) * pl.num_programs(1) + pl.program_id(1)
    half = xg_ref.shape[1]
    rows_per_step = cap // n_d

    def row_copy(g, r):
        return pltpu.make_async_copy(hm_hbm.at[pl.ds(rows_ref[g * cap + r], 1), :],
                                     xg_ref.at[pl.ds(r, 1), :], sem)

    def wait_gather(g):
        def body(r, c):
            row_copy(g, r).wait()
            return c
        lax.fori_loop(0, cap, body, 0, unroll=8)

    @pl.when(j == 0)
    def _():
        @pl.when(group == 0)
        def _():
            def body(r, c):
                row_copy(group, r).start()
                return c
            lax.fori_loop(0, cap, body, 0, unroll=8)

        wait_gather(group)
        lo, hi = _unpack_halves(xg_ref[...])
        xb_ref[:, :half] = lo
        xb_ref[:, half:] = hi

    @pl.when(j < n_f)
    def _():
        x = xb_ref[...]
        a = jnp.dot(x, wg_ref[0].astype(BF16), preferred_element_type=F32)
        u = jnp.dot(x, wu_ref[0].astype(BF16), preferred_element_type=F32)
        hval = (a * _sigmoid(a) * u).astype(BF16)
        for jj in range(n_f):
            @pl.when(j == jj)
            def _():
                hmid_ref[:, jj * tf:(jj + 1) * tf] = hval

    @pl.when(j >= n_f)
    def _():
        nxt = jnp.minimum(group + 1, n_groups - 1)
        base = (j - n_f) * rows_per_step
        for r in range(rows_per_step):
            row_copy(nxt, base + r).start()
        y = jnp.dot(hmid_ref[...], wd_ref[0].astype(BF16), preferred_element_type=F32)
        y_ref[0, 0, :cap, :] = y.astype(y_ref.dtype)
        y_ref[0, 0, cap:, :] = jnp.zeros((y_ref.shape[2] - cap, y_ref.shape[3]), y_ref.dtype)

    @pl.when((group == n_groups - 1) & (j == n_f + n_d - 1))
    def _():
        wait_gather(group)


def _experts(rows, hm_packed, wg, wu, wd, bsz, cap):
    ne, d, ff = wg.shape
    tf = _pick((256, 128), ff)
    td = _pick((512, 256, 128), d)
    n_f = ff // tf
    n_d = d // td
    cp = cap + COMBINE_WIN
    grid_spec = pltpu.PrefetchScalarGridSpec(
        num_scalar_prefetch=1,
        grid=(ne, bsz, n_f + n_d),
        in_specs=[pl.BlockSpec(memory_space=pl.ANY),
                  pl.BlockSpec((1, d, tf), lambda e, b, j, *_: (e, 0, jnp.minimum(j, n_f - 1))),
                  pl.BlockSpec((1, d, tf), lambda e, b, j, *_: (e, 0, jnp.minimum(j, n_f - 1))),
                  pl.BlockSpec((1, ff, td), lambda e, b, j, *_: (e, 0, jnp.maximum(j - n_f, 0)))],
        out_specs=pl.BlockSpec((1, 1, cp, td), lambda e, b, j, *_: (e, b, 0, jnp.maximum(j - n_f, 0))),
        scratch_shapes=[pltpu.VMEM((cap, d // 2), jnp.uint32), pltpu.VMEM((cap, d), BF16),
                        pltpu.VMEM((cap, ff), BF16), pltpu.SemaphoreType.DMA(())],
    )
    return pl.pallas_call(
        functools.partial(_expert_kernel, cap=cap, n_f=n_f, n_d=n_d, tf=tf, n_groups=ne * bsz),
        out_shape=jax.ShapeDtypeStruct((ne, bsz, cp, d), BF16),
        grid_spec=grid_spec,
        compiler_params=_cparams(("arbitrary", "arbitrary", "arbitrary")),
        name="expert_ffn",
    )(rows, hm_packed, wg, wu, wd)


def _combine_kernel(start_ref, small_ref, x2_ref, pos_ref, aff_ref, g_ref, y_hbm, o_ref,
                    fast_ref, slow_ref, fast_sem, slow_sem, *, ne, nb, bsz):
    b = pl.program_id(0)
    j = pl.program_id(1)
    step = b * nb + j
    total = bsz * nb
    d = fast_ref.shape[3]

    def window_copies(st, win, dst, sem):
        bb = st // nb
        jj = st - bb * nb
        return [pltpu.make_async_copy(
            y_hbm.at[e, bb, pl.ds(pl.multiple_of(start_ref[(bb * ne + e) * nb + jj], BF16_SUBLANES), win), :],
            dst.at[e], sem) for e in range(ne)]

    def small_copies(st):
        slot = st % 2
        return window_copies(st, COMBINE_WIN_SMALL, fast_ref.at[slot], fast_sem.at[slot])

    @pl.when((step == 0) & (small_ref[0] == 1))
    def _():
        for c in small_copies(step):
            c.start()

    nxt = jnp.minimum(step + 1, total - 1)

    @pl.when((step + 1 < total) & (small_ref[nxt] == 1))
    def _():
        for c in small_copies(nxt):
            c.start()

    pos = pos_ref[0]
    gate = aff_ref[0]

    def finish(win, wins):
        lane = lax.broadcasted_iota(jnp.int32, (1, win), 1)
        parts = []
        for e in range(ne):
            rel = pos[:, e:e + 1] - start_ref[(b * ne + e) * nb + j]
            parts.append(jnp.where(rel == lane, gate[:, e:e + 1], 0.0).astype(BF16))
        sel = jnp.concatenate(parts, axis=1)
        x3 = x2_ref[...] + jnp.dot(sel, wins.reshape(ne * win, d), preferred_element_type=F32)
        o_ref[...] = _rms(x3, g_ref[...])

    @pl.when(small_ref[step] == 1)
    def _():
        for c in small_copies(step):
            c.wait()
        finish(COMBINE_WIN_SMALL, fast_ref[step % 2])

    @pl.when(small_ref[step] != 1)
    def _():
        copies = window_copies(step, COMBINE_WIN, slow_ref, slow_sem)
        for c in copies:
            c.start()
        for c in copies:
            c.wait()
        finish(COMBINE_WIN, slow_ref[...])


def _combine(starts, small, x2, pos, aff, g_final, y, seq):
    m, d = x2.shape
    ne, bsz = y.shape[0], y.shape[1]
    nb = seq // BLOCK
    grid_spec = pltpu.PrefetchScalarGridSpec(
        num_scalar_prefetch=2,
        grid=(bsz, nb),
        in_specs=[pl.BlockSpec((BLOCK, d), lambda b, j, *_: (b * nb + j, 0)),
                  pl.BlockSpec((1, BLOCK, ne), lambda b, j, *_: (b, j, 0)),
                  pl.BlockSpec((1, BLOCK, ne), lambda b, j, *_: (b, j, 0)),
                  pl.BlockSpec((1, d), lambda b, j, *_: (0, 0)),
                  pl.BlockSpec(memory_space=pl.ANY)],
        out_specs=pl.BlockSpec((BLOCK, d), lambda b, j, *_: (b * nb + j, 0)),
        scratch_shapes=[pltpu.VMEM((2, ne, COMBINE_WIN_SMALL, d), BF16), pltpu.VMEM((ne, COMBINE_WIN, d), BF16),
                        pltpu.SemaphoreType.DMA((2,)), pltpu.SemaphoreType.DMA(())],
    )
    return pl.pallas_call(
        functools.partial(_combine_kernel, ne=ne, nb=nb, bsz=bsz),
        out_shape=jax.ShapeDtypeStruct((m, d), F32),
        grid_spec=grid_spec,
        compiler_params=_cparams(("arbitrary", "arbitrary")),
        name="moe_combine_final_norm",
    )(starts, small, x2, pos, aff, g_final.reshape(1, d).astype(F32), y)


def kernel(x, mem, g_mix, w_in, conv_w, attn_sinks, w_conv_out, w_attn_out, w_out, g_cross, g_mem, w_q_cross,
           w_kv_cross, w_o_cross, g_moe, w_router, w_gate_e, w_up_e, w_down_e, g_final):
    bsz, seq, d = x.shape
    m = bsz * seq
    cw = conv_w.shape[1]
    aw = w_attn_out.shape[0]
    n_heads = attn_sinks.shape[0]
    in_cols = w_in.shape[1]
    kvw = (in_cols - 3 * cw - aw - 2 * d) // 2
    off_b, off_c, off_u = 0, cw, 2 * cw
    off_q = 3 * cw
    off_k = off_q + aw
    off_v = off_k + kvw
    off_ga = off_v + kvw
    off_gb = off_ga + d
    ne = w_router.shape[1]
    cap = CAPACITY_FACTOR * seq // ne

    x2d = x.reshape(m, d)

    h = _rmsnorm(x2d, g_mix, BF16)
    p = _inproj(h, w_in, off_ga)
    za = _conv_mixer(p, conv_w, seq, off_b, off_c, off_u)
    slopes = jnp.power(2.0, -8.0 * (jnp.arange(n_heads, dtype=F32) + 1.0) / n_heads)
    ob = _windowed_attention(p, slopes, attn_sinks.astype(F32), bsz, seq, off_q, off_k, off_v, aw, kvw, n_heads)
    mix = _merge(za, ob, w_conv_out, w_attn_out, p, off_ga, off_gb)
    x1 = _resid_matmul(mix, w_out, x2d)

    mem_len = mem.shape[1]
    kv = _mem_kv(mem.reshape(bsz * mem_len, d), g_mem, w_kv_cross.astype(BF16)).reshape(bsz, mem_len, -1)
    wr_hi = w_router.astype(BF16)
    wr_lo = (w_router - wr_hi.astype(F32)).astype(BF16)
    x2, hm_packed, logits = _cross_attention(x1, g_cross, w_q_cross.astype(BF16), kv, w_o_cross.astype(BF16),
                                             g_moe, jnp.concatenate([wr_hi, wr_lo], axis=1), seq)

    out = _moe_and_final_norm(x2, hm_packed, logits.reshape(bsz, seq, ne), w_gate_e, w_up_e, w_down_e, g_final)
    return out.reshape(bsz, seq, d)


def _moe_and_final_norm(x2, hm_packed, logits, w_gate_e, w_up_e, w_down_e, g_final):
    bsz, seq, ne = logits.shape
    cap = CAPACITY_FACTOR * seq // ne
    pos, aff, csel, ends = _route(logits, cap)
    idx = _slot_tokens(csel, cap)
    rows = (idx + (jnp.arange(bsz, dtype=jnp.int32) * seq)[:, None, None]).transpose(1, 0, 2).reshape(-1)
    y = _experts(rows, hm_packed, w_gate_e, w_up_e, w_down_e, bsz, cap)
    first = jnp.concatenate([jnp.zeros((bsz, 1, ne), jnp.int32), ends[:, :-1, :]], axis=1)
    starts = (first // BF16_SUBLANES) * BF16_SUBLANES
    small = jnp.all(ends - starts <= COMBINE_WIN_SMALL, axis=2).astype(jnp.int32).reshape(-1)
    return _combine(starts.transpose(0, 2, 1).reshape(-1), small, x2, pos, aff, g_final, y, seq)
```

```python
import functools
import math

import jax
import jax.numpy as jnp
from jax import lax
from jax.experimental import pallas as pl
from jax.experimental.pallas import tpu as pltpu

RMS_EPS = 1e-6
NEG_INF = -1e30
LOG2E = math.log2(math.e)
MIN_ALIBI_SLOPE = 2.0 ** -8
MASKED_DIST = -NEG_INF / MIN_ALIBI_SLOPE
WINDOW = 128
BLOCK = 128
ROUTE_CHUNK = 256
N_CROSS_HEADS = 4
CAPACITY_FACTOR = 2
BF16_SUBLANES = 16
COMBINE_WIN = BLOCK + BF16_SUBLANES
COMBINE_BLOCKS = 2
COMBINE_WIN_SMALL = 4 * BF16_SUBLANES
COMBINE_SLOW_EXPERTS = 8
VMEM_LIMIT_BYTES = 56 * 1024 * 1024

F32 = jnp.float32
BF16 = jnp.bfloat16


def _cparams(semantics, vmem=VMEM_LIMIT_BYTES):
    return pltpu.CompilerParams(dimension_semantics=semantics, vmem_limit_bytes=vmem)


def _pick(prefs, *sizes):
    for t in prefs:
        if all(s % t == 0 for s in sizes):
            return t
    raise ValueError(f"no tile in {prefs} divides {sizes}")


def _rms(x, g):
    return x * lax.rsqrt(jnp.mean(x * x, axis=-1, keepdims=True) + RMS_EPS) * g


def _sigmoid(x):
    return 1.0 / (1.0 + jnp.exp(-x))


def _rmsnorm_kernel(x_ref, g_ref, o_ref):
    o_ref[...] = _rms(x_ref[...].astype(F32), g_ref[...]).astype(o_ref.dtype)


def _rmsnorm(x2d, g, out_dtype):
    m, d = x2d.shape
    tm = _pick((512, 256, 128, 64, 8), m)
    return pl.pallas_call(
        _rmsnorm_kernel,
        out_shape=jax.ShapeDtypeStruct((m, d), out_dtype),
        grid=(m // tm,),
        in_specs=[pl.BlockSpec((tm, d), lambda i: (i, 0)), pl.BlockSpec((1, d), lambda i: (0, 0))],
        out_specs=pl.BlockSpec((tm, d), lambda i: (i, 0)),
        compiler_params=_cparams(("parallel",)),
        name="rmsnorm",
    )(x2d, g.reshape(1, d).astype(F32))


def _inproj_kernel(h_ref, w_hbm, o_ref, stage_ref, wb_ref, sem, *, gate_tile0, tn, n_tiles):
    j = pl.program_id(0)

    def tile_copy(jj):
        return pltpu.make_async_copy(w_hbm.at[:, pl.ds(pl.multiple_of(jj * tn, tn), tn)], stage_ref, sem)

    @pl.when(pl.program_id(1) == 0)
    def _():
        @pl.when(j == 0)
        def _():
            tile_copy(j).start()

        tile_copy(j).wait()
        wb_ref[...] = stage_ref[...].astype(BF16)

        @pl.when(j + 1 < n_tiles)
        def _():
            tile_copy(j + 1).start()

    acc = jnp.dot(h_ref[...], wb_ref[...], preferred_element_type=F32)

    @pl.when(j < gate_tile0)
    def _():
        o_ref[...] = acc.astype(o_ref.dtype)

    @pl.when(j >= gate_tile0)
    def _():
        o_ref[...] = _sigmoid(acc).astype(o_ref.dtype)


def _inproj(h, w, off_gate):
    m, d = h.shape
    n = w.shape[1]
    tm = _pick((1024, 512, 256, 128), m)
    tn = _pick((1024, 512, 256, 128), n, off_gate)
    return pl.pallas_call(
        functools.partial(_inproj_kernel, gate_tile0=off_gate // tn, tn=tn, n_tiles=n // tn),
        out_shape=jax.ShapeDtypeStruct((m, n), BF16),
        grid=(n // tn, m // tm),
        in_specs=[pl.BlockSpec((tm, d), lambda j, i: (i, 0)), pl.BlockSpec(memory_space=pl.ANY)],
        out_specs=pl.BlockSpec((tm, tn), lambda j, i: (i, j)),
        scratch_shapes=[pltpu.VMEM((d, tn), F32), pltpu.VMEM((d, tn), BF16), pltpu.SemaphoreType.DMA(())],
        compiler_params=_cparams(("arbitrary", "arbitrary")),
        name="inproj",
    )(h, w)


def _conv_kernel(b_ref, c_ref, u_ref, cp_ref, up_ref, cn_ref, un_ref, w_ref, o_ref, *, seq_blocks):
    i = pl.program_id(0)
    tm = c_ref.shape[0]
    z = c_ref[...].astype(F32) * u_ref[...].astype(F32)
    last_row = BF16_SUBLANES - 1
    z_prev = cp_ref[last_row:last_row + 1, :].astype(F32) * up_ref[last_row:last_row + 1, :].astype(F32)
    z_next = cn_ref[0:1, :].astype(F32) * un_ref[0:1, :].astype(F32)
    pos = i % seq_blocks
    z_prev = jnp.where(pos == 0, 0.0, z_prev)
    z_next = jnp.where(pos == seq_blocks - 1, 0.0, z_next)
    row = lax.broadcasted_iota(jnp.int32, z.shape, 0)
    z_up = jnp.where(row == 0, z_prev, pltpu.roll(z, 1, axis=0))
    z_dn = jnp.where(row == tm - 1, z_next, pltpu.roll(z, tm - 1, axis=0))
    w = w_ref[...]
    conv = w[0:1, :] * z_up + w[1:2, :] * z + w[2:3, :] * z_dn
    o_ref[...] = (b_ref[...].astype(F32) * conv).astype(o_ref.dtype)


def _conv_mixer(p, conv_w, seq, off_b, off_c, off_u):
    m = p.shape[0]
    cw = conv_w.shape[1]
    tm = _pick((512, 256, 128), seq)
    tc = _pick((1024, 512, 256, 128), cw, off_b, off_c, off_u)
    halo = BF16_SUBLANES
    rb = tm // halo
    n_halo = m // halo

    def main(off):
        return pl.BlockSpec((tm, tc), lambda i, j: (i, off // tc + j))

    def prev(off):
        return pl.BlockSpec((halo, tc), lambda i, j: (jnp.maximum(i * rb - 1, 0), off // tc + j))

    def nxt(off):
        return pl.BlockSpec((halo, tc), lambda i, j: (jnp.minimum((i + 1) * rb, n_halo - 1), off // tc + j))

    return pl.pallas_call(
        functools.partial(_conv_kernel, seq_blocks=seq // tm),
        out_shape=jax.ShapeDtypeStruct((m, cw), BF16),
        grid=(m // tm, cw // tc),
        in_specs=[main(off_b), main(off_c), main(off_u), prev(off_c), prev(off_u), nxt(off_c), nxt(off_u),
                  pl.BlockSpec((conv_w.shape[0], tc), lambda i, j: (0, j))],
        out_specs=pl.BlockSpec((tm, tc), lambda i, j: (i, j)),
        compiler_params=_cparams(("parallel", "parallel")),
        name="conv_mixer",
    )(p, p, p, p, p, p, p, conv_w.astype(F32))


def _swa_kernel(slopes_ref, sinks_ref, q_ref, kp_ref, kc_ref, kn_ref, vp_ref, vc_ref, vn_ref, o_ref,
                *, n_kv, group, hd, nb):
    n = pl.program_id(1)
    has_prev = n > 0
    has_next = n < nb - 1
    shape = (BLOCK, 3 * BLOCK)
    krel = lax.broadcasted_iota(jnp.int32, shape, 1) - BLOCK
    dist = jnp.abs(lax.broadcasted_iota(jnp.int32, shape, 0) - krel)
    valid = (dist <= WINDOW) & ((krel >= 0) | has_prev) & ((krel < BLOCK) | has_next)
    masked_dist = jnp.where(valid, dist.astype(F32), MASKED_DIST)[None]
    scale = LOG2E / math.sqrt(hd)
    head_in_group = lax.broadcasted_iota(jnp.int32, (group, 1, 1), 0)
    for kh in range(n_kv):
        cols = slice(kh * hd, (kh + 1) * hd)
        kband = jnp.concatenate([kp_ref[:, cols], kc_ref[:, cols], kn_ref[:, cols]], axis=0)
        vband = jnp.concatenate([vp_ref[:, cols], vc_ref[:, cols], vn_ref[:, cols]], axis=0)
        heads = [kh * group + gi for gi in range(group)]
        q = jnp.concatenate([q_ref[:, h * hd:(h + 1) * hd] for h in heads], axis=0)
        slope = jnp.zeros((group, 1, 1), F32)
        sink = jnp.zeros((group, 1, 1), F32)
        for gi, h in enumerate(heads):
            slope = jnp.where(head_in_group == gi, slopes_ref[h] * LOG2E, slope)
            sink = jnp.where(head_in_group == gi, sinks_ref[h] * LOG2E, sink)
        s = lax.dot_general(q, kband, (((1,), (1,)), ((), ())), preferred_element_type=F32) * scale
        s = s.reshape(group, BLOCK, 3 * BLOCK) - slope * masked_dist
        mx = jnp.maximum(jnp.max(s, axis=-1, keepdims=True), sink)
        pr = jnp.exp2(s - mx)
        denom = jnp.sum(pr, axis=-1, keepdims=True) + jnp.exp2(sink - mx)
        o = jnp.dot(pr.reshape(group * BLOCK, 3 * BLOCK).astype(BF16), vband, preferred_element_type=F32)
        o = o.reshape(group, BLOCK, hd) / denom
        for gi, h in enumerate(heads):
            o_ref[:, h * hd:(h + 1) * hd] = o[gi].astype(o_ref.dtype)


def _windowed_attention(p, slopes, sinks, bsz, seq, off_q, off_k, off_v, aw, kvw, n_heads):
    m = p.shape[0]
    hd = aw // n_heads
    n_kv = kvw // hd
    nb = seq // BLOCK
    assert off_q % aw == 0 and off_k % kvw == 0 and off_v % kvw == 0 and seq % BLOCK == 0

    def kv_spec(off, shift):
        def imap(b, n, *_):
            return (b * nb + jnp.clip(n + shift, 0, nb - 1), off // kvw)
        return pl.BlockSpec((BLOCK, kvw), imap)

    grid_spec = pltpu.PrefetchScalarGridSpec(
        num_scalar_prefetch=2,
        grid=(bsz, nb),
        in_specs=[pl.BlockSpec((BLOCK, aw), lambda b, n, *_: (b * nb + n, off_q // aw)),
                  kv_spec(off_k, -1), kv_spec(off_k, 0), kv_spec(off_k, 1),
                  kv_spec(off_v, -1), kv_spec(off_v, 0), kv_spec(off_v, 1)],
        out_specs=pl.BlockSpec((BLOCK, aw), lambda b, n, *_: (b * nb + n, 0)),
    )
    return pl.pallas_call(
        functools.partial(_swa_kernel, n_kv=n_kv, group=n_heads // n_kv, hd=hd, nb=nb),
        out_shape=jax.ShapeDtypeStruct((m, aw), BF16),
        grid_spec=grid_spec,
        compiler_params=_cparams(("parallel", "parallel")),
        name="windowed_gqa",
    )(slopes, sinks, p, p, p, p, p, p, p)


def _merge_kernel(za_ref, ob_ref, wc_ref, wa_ref, ga_ref, gb_ref, o_ref, wcb_ref, wab_ref):
    @pl.when(pl.program_id(1) == 0)
    def _():
        wcb_ref[...] = wc_ref[...].astype(BF16)
        wab_ref[...] = wa_ref[...].astype(BF16)

    ya = jnp.dot(za_ref[...], wcb_ref[...], preferred_element_type=F32)
    yb = jnp.dot(ob_ref[...], wab_ref[...], preferred_element_type=F32)
    o_ref[...] = (ga_ref[...].astype(F32) * ya + gb_ref[...].astype(F32) * yb).astype(o_ref.dtype)


def _merge(za, ob, wc, wa, p, off_ga, off_gb):
    m, ka = za.shape
    kb = ob.shape[1]
    d = wc.shape[1]
    tm = _pick((1024, 512, 256, 128), m)
    tn = _pick((512, 256, 128), d, off_ga, off_gb)
    return pl.pallas_call(
        _merge_kernel,
        out_shape=jax.ShapeDtypeStruct((m, d), BF16),
        grid=(d // tn, m // tm),
        in_specs=[pl.BlockSpec((tm, ka), lambda j, i: (i, 0)), pl.BlockSpec((tm, kb), lambda j, i: (i, 0)),
                  pl.BlockSpec((ka, tn), lambda j, i: (0, j)), pl.BlockSpec((kb, tn), lambda j, i: (0, j)),
                  pl.BlockSpec((tm, tn), lambda j, i: (i, off_ga // tn + j)),
                  pl.BlockSpec((tm, tn), lambda j, i: (i, off_gb // tn + j))],
        out_specs=pl.BlockSpec((tm, tn), lambda j, i: (i, j)),
        scratch_shapes=[pltpu.VMEM((ka, tn), BF16), pltpu.VMEM((kb, tn), BF16)],
        compiler_params=_cparams(("arbitrary", "arbitrary")),
        name="gated_merge",
    )(za, ob, wc, wa, p, p)


def _resid_matmul_kernel(a_ref, w_ref, x_ref, o_ref, wb_ref):
    @pl.when(pl.program_id(1) == 0)
    def _():
        wb_ref[...] = w_ref[...].astype(BF16)

    o_ref[...] = x_ref[...] + jnp.dot(a_ref[...], wb_ref[...], preferred_element_type=F32)


def _resid_matmul(a, w, x):
    m, k = a.shape
    n = w.shape[1]
    tm = _pick((1024, 512, 256, 128), m)
    tn = _pick((512, 256, 128), n)
    return pl.pallas_call(
        _resid_matmul_kernel,
        out_shape=jax.ShapeDtypeStruct((m, n), F32),
        grid=(n // tn, m // tm),
        in_specs=[pl.BlockSpec((tm, k), lambda j, i: (i, 0)), pl.BlockSpec((k, tn), lambda j, i: (0, j)),
                  pl.BlockSpec((tm, tn), lambda j, i: (i, j))],
        out_specs=pl.BlockSpec((tm, tn), lambda j, i: (i, j)),
        scratch_shapes=[pltpu.VMEM((k, tn), BF16)],
        compiler_params=_cparams(("arbitrary", "arbitrary")),
        name="out_proj_residual",
    )(a, w, x)


def _mem_kv_kernel(mem_ref, g_ref, w_ref, o_ref):
    mn = _rms(mem_ref[...].astype(F32), g_ref[...]).astype(BF16)
    o_ref[...] = jnp.dot(mn, w_ref[...], preferred_element_type=F32).astype(o_ref.dtype)


def _mem_kv(mem2d, g, w):
    m, d = mem2d.shape
    n = w.shape[1]
    tn = _pick((512, 256, 128), n)
    return pl.pallas_call(
        _mem_kv_kernel,
        out_shape=jax.ShapeDtypeStruct((m, n), BF16),
        grid=(n // tn,),
        in_specs=[pl.BlockSpec((m, d), lambda j: (0, 0)), pl.BlockSpec((1, d), lambda j: (0, 0)),
                  pl.BlockSpec((d, tn), lambda j: (0, j))],
        out_specs=pl.BlockSpec((m, tn), lambda j: (0, j)),
        compiler_params=_cparams(("parallel",)),
        name="mem_kv",
    )(mem2d, g.reshape(1, d).astype(F32), w)


def _pack_halves(hb):
    half = hb.shape[1] // 2
    lo = lax.bitcast_convert_type(hb[:, :half].astype(F32), jnp.uint32)
    hi = lax.bitcast_convert_type(hb[:, half:].astype(F32), jnp.uint32)
    return (lo >> 16) | (hi & jnp.uint32(0xFFFF0000))


def _unpack_halves(w):
    lo = lax.bitcast_convert_type(w << 16, F32).astype(BF16)
    hi = lax.bitcast_convert_type(w & jnp.uint32(0xFFFF0000), F32).astype(BF16)
    return lo, hi


def _cross_kernel(x_ref, gc_ref, wq_ref, kv_ref, wo_ref, gm_ref, wrc_ref, x2_ref, hm_ref, lg_ref, *, cw):
    x = x_ref[...]
    hc = _rms(x, gc_ref[...]).astype(BF16)
    q = jnp.dot(hc, wq_ref[...], preferred_element_type=F32).astype(BF16)
    hd = cw // N_CROSS_HEADS
    inv = 1.0 / math.sqrt(hd)
    outs = []
    for h in range(N_CROSS_HEADS):
        k = kv_ref[0, :, h * hd:(h + 1) * hd]
        v = kv_ref[0, :, cw + h * hd:cw + (h + 1) * hd]
        s = lax.dot_general(q[:, h * hd:(h + 1) * hd], k, (((1,), (1,)), ((), ())),
                            preferred_element_type=F32) * inv
        pr = jnp.exp(s - jnp.max(s, axis=-1, keepdims=True))
        o = jnp.dot(pr.astype(BF16), v, preferred_element_type=F32) / jnp.sum(pr, axis=-1, keepdims=True)
        outs.append(o.astype(BF16))
    o = jnp.concatenate(outs, axis=1)
    x2 = x + jnp.dot(o, wo_ref[...], preferred_element_type=F32)
    x2_ref[...] = x2
    hm = _rms(x2, gm_ref[...])
    hi = hm.astype(BF16)
    lo = (hm - hi.astype(F32)).astype(BF16)
    hm_ref[...] = _pack_halves(hi)
    ne = lg_ref.shape[1]
    both = jnp.dot(hi, wrc_ref[...], preferred_element_type=F32)
    lg_ref[...] = both[:, :ne] + both[:, ne:] + jnp.dot(lo, wrc_ref[:, :ne], preferred_element_type=F32)


def _cross_attention(x1, g_cross, wq, kv, wo, g_moe, wr_cat, seq):
    m, d = x1.shape
    cw = wq.shape[1]
    ne = wr_cat.shape[1] // 2
    mem_len = kv.shape[1]
    tm = _pick((256, 128), seq)
    per_seq = seq // tm
    const = lambda i: (0, 0)
    return pl.pallas_call(
        functools.partial(_cross_kernel, cw=cw),
        out_shape=(jax.ShapeDtypeStruct((m, d), F32), jax.ShapeDtypeStruct((m, d // 2), jnp.uint32),
                   jax.ShapeDtypeStruct((m, ne), F32)),
        grid=(m // tm,),
        in_specs=[pl.BlockSpec((tm, d), lambda i: (i, 0)), pl.BlockSpec((1, d), const),
                  pl.BlockSpec((d, cw), const),
                  pl.BlockSpec((1, mem_len, 2 * cw), lambda i: (i // per_seq, 0, 0)),
                  pl.BlockSpec((cw, d), const), pl.BlockSpec((1, d), const),
                  pl.BlockSpec((d, 2 * ne), const)],
        out_specs=(pl.BlockSpec((tm, d), lambda i: (i, 0)), pl.BlockSpec((tm, d // 2), lambda i: (i, 0)),
                   pl.BlockSpec((tm, ne), lambda i: (i, 0))),
        compiler_params=_cparams(("parallel",)),
        name="cross_attention",
    )(x1, g_cross.reshape(1, d).astype(F32), wq, kv, wo, g_moe.reshape(1, d).astype(F32), wr_cat)


def _route_kernel(lg_ref, pos_ref, aff_ref, csel_ref, ends_ref, gt_ref, eq_ref, cgt_ref, ceq_ref, *, cap, chunk):
    logits = lg_ref[0]
    t_len, ne = logits.shape
    ex = jnp.exp(logits - jnp.max(logits, axis=-1, keepdims=True))
    aff = ex / jnp.sum(ex, axis=-1, keepdims=True)
    aff_ref[0] = aff
    keys = lax.bitcast_convert_type(aff, jnp.int32)

    def search(it, thr):
        cand = thr | jnp.left_shift(jnp.int32(1), 30 - it)
        cnt = jnp.sum((keys >= cand).astype(jnp.int32), axis=0, keepdims=True)
        return jnp.where(cnt >= cap, cand, thr)

    thr = lax.fori_loop(0, 31, search, jnp.zeros((1, ne), jnp.int32))
    gt = keys > thr
    eq = keys == thr
    ties_taken = (cap - jnp.sum(gt.astype(jnp.int32), axis=0, keepdims=True)).astype(F32)
    gt_ref[...] = jnp.where(gt, 1.0, 0.0).astype(BF16)
    eq_ref[...] = jnp.where(eq, 1.0, 0.0).astype(BF16)

    tri = (lax.broadcasted_iota(jnp.int32, (chunk, chunk), 0)
           >= lax.broadcasted_iota(jnp.int32, (chunk, chunk), 1)).astype(BF16)

    def prefix(ci, carry):
        rows = pl.ds(pl.multiple_of(ci * chunk, chunk), chunk)
        cg = jnp.dot(tri, gt_ref[rows, :], preferred_element_type=F32) + carry[0]
        ce = jnp.dot(tri, eq_ref[rows, :], preferred_element_type=F32) + carry[1]
        cgt_ref[rows, :] = cg
        ceq_ref[rows, :] = ce
        return cg[chunk - 1:chunk, :], ce[chunk - 1:chunk, :]

    zero = jnp.zeros((1, ne), F32)
    lax.fori_loop(0, t_len // chunk, prefix, (zero, zero))
    ceq = ceq_ref[...]
    sel = gt | (eq & (ceq <= ties_taken))
    csel = (cgt_ref[...] + jnp.minimum(ceq, ties_taken)).astype(jnp.int32)
    csel_ref[0] = csel
    pos_ref[0] = jnp.where(sel, csel - 1, -1)
    ends_ref[0] = csel_ref[0, pl.ds(BLOCK - 1, t_len // BLOCK, stride=BLOCK), :]


def _route(logits, cap):
    bsz, t_len, ne = logits.shape
    chunk = _pick((ROUTE_CHUNK, BLOCK), t_len)
    nb = t_len // BLOCK
    per_seq = lambda b: (b, 0, 0)
    return pl.pallas_call(
        functools.partial(_route_kernel, cap=cap, chunk=chunk),
        out_shape=(jax.ShapeDtypeStruct((bsz, t_len, ne), jnp.int32),
                   jax.ShapeDtypeStruct((bsz, t_len, ne), F32),
                   jax.ShapeDtypeStruct((bsz, t_len, ne), jnp.int32),
                   jax.ShapeDtypeStruct((bsz, nb, ne), jnp.int32)),
        grid=(bsz,),
        in_specs=[pl.BlockSpec((1, t_len, ne), per_seq)],
        out_specs=(pl.BlockSpec((1, t_len, ne), per_seq), pl.BlockSpec((1, t_len, ne), per_seq),
                   pl.BlockSpec((1, t_len, ne), per_seq), pl.BlockSpec((1, nb, ne), per_seq)),
        scratch_shapes=[pltpu.VMEM((t_len, ne), BF16), pltpu.VMEM((t_len, ne), BF16),
                        pltpu.VMEM((t_len, ne), F32), pltpu.VMEM((t_len, ne), F32)],
        compiler_params=_cparams(("parallel",)),
        name="expert_choice_route",
    )(logits)


def _slot_tokens_kernel(lo_ref, hi_ref, csel_ref, idx_ref, part_ref, *, cap, chunk, tw):
    b = pl.program_id(0)
    ne = csel_ref.shape[1]
    n_tiles = cap // tw
    lane_tiles = chunk // BLOCK
    ones = jnp.ones((8, BLOCK), BF16)
    for e in range(ne):
        full = []
        for ct in range(n_tiles):
            slot = lax.broadcasted_iota(jnp.int32, (tw, 1), 0) + ct * tw
            lo = lo_ref[(b * ne + e) * n_tiles + ct]
            hi = hi_ref[(b * ne + e) * n_tiles + ct]

            def count(ci, acc):
                le = jnp.where(csel_ref[0, e, pl.ds(ci, 1), :] <= slot, 1.0, 0.0)
                for lt in range(lane_tiles):
                    acc = acc + le[:, lt * BLOCK:(lt + 1) * BLOCK]
                return acc

            acc = lax.fori_loop(lo, hi, count, jnp.zeros((tw, BLOCK), F32))
            part_ref[ct * tw:(ct + 1) * tw, :] = acc.astype(BF16)
            full.append(jnp.full((1, tw), lo * chunk, jnp.int32))
        total = lax.dot_general(ones, part_ref[...], (((1,), (1,)), ((), ())), preferred_element_type=F32)
        idx_ref[0, e:e + 1, :] = total[0:1, :].astype(jnp.int32) + jnp.concatenate(full, axis=1)


def _slot_tokens(csel, cap):
    bsz, t_len, ne = csel.shape
    chunk = _pick((ROUTE_CHUNK, BLOCK), t_len)
    tw = math.gcd(cap, BLOCK)
    tile_first = jnp.arange(cap // tw, dtype=jnp.int32) * tw
    chunk_first = csel[:, 0::chunk, :, None]
    chunk_last = csel[:, chunk - 1::chunk, :, None]
    lo = jnp.sum(chunk_last <= tile_first, axis=1, dtype=jnp.int32)
    hi = jnp.sum(chunk_first <= tile_first + (tw - 1), axis=1, dtype=jnp.int32)
    n_chunks = t_len // chunk
    assert t_len // BLOCK <= 256
    csel_t = csel.transpose(0, 2, 1).reshape(bsz, ne, n_chunks, chunk)
    grid_spec = pltpu.PrefetchScalarGridSpec(
        num_scalar_prefetch=2,
        grid=(bsz,),
        in_specs=[pl.BlockSpec((1, ne, n_chunks, chunk), lambda b, *_: (b, 0, 0, 0))],
        out_specs=pl.BlockSpec((1, ne, cap), lambda b, *_: (b, 0, 0)),
        scratch_shapes=[pltpu.VMEM((cap, BLOCK), BF16)],
    )
    return pl.pallas_call(
        functools.partial(_slot_tokens_kernel, cap=cap, chunk=chunk, tw=tw),
        out_shape=jax.ShapeDtypeStruct((bsz, ne, cap), jnp.int32),
        grid_spec=grid_spec,
        compiler_params=_cparams(("parallel",)),
        name="slot_tokens",
    )(lo.reshape(-1), hi.reshape(-1), csel_t)


def _expert_kernel(rows_ref, hm_hbm, wg_ref, wu_ref, wd_ref, y_ref, xg_ref, xb_ref, hmid_ref, sem,
                   *, cap, n_f, n_d, tf, n_groups):
    j = pl.program_id(2)
    group = pl.program_id(0) * pl.num_programs(1) + pl.program_id(1)
    half = xg_ref.shape[1]
    rows_per_step = cap // n_d

    def row_copy(g, r):
        return pltpu.make_async_copy(hm_hbm.at[pl.ds(rows_ref[g * cap + r], 1), :],
                                     xg_ref.at[pl.ds(r, 1), :], sem)

    def wait_gather(g):
        def body(r, c):
            row_copy(g, r).wait()
            return c
        lax.fori_loop(0, cap, body, 0, unroll=8)

    @pl.when(j == 0)
    def _():
        @pl.when(group == 0)
        def _():
            def body(r, c):
                row_copy(group, r).start()
                return c
            lax.fori_loop(0, cap, body, 0, unroll=8)

        wait_gather(group)
        lo, hi = _unpack_halves(xg_ref[...])
        xb_ref[:, :half] = lo
        xb_ref[:, half:] = hi

    @pl.when(j < n_f)
    def _():
        x = xb_ref[...]
        a = jnp.dot(x, wg_ref[0].astype(BF16), preferred_element_type=F32)
        u = jnp.dot(x, wu_ref[0].astype(BF16), preferred_element_type=F32)
        hval = (a * _sigmoid(a) * u).astype(BF16)
        for jj in range(n_f):
            @pl.when(j == jj)
            def _():
                hmid_ref[:, jj * tf:(jj + 1) * tf] = hval

    @pl.when(j >= n_f)
    def _():
        nxt = jnp.minimum(group + 1, n_groups - 1)
        base = (j - n_f) * rows_per_step
        for r in range(rows_per_step):
            row_copy(nxt, base + r).start()
        y = jnp.dot(hmid_ref[...], wd_ref[0].astype(BF16), preferred_element_type=F32)
        y_ref[0, 0, :cap, :] = y.astype(y_ref.dtype)
        y_ref[0, 0, cap:, :] = jnp.zeros((y_ref.shape[2] - cap, y_ref.shape[3]), y_ref.dtype)

    @pl.when((group == n_groups - 1) & (j == n_f + n_d - 1))
    def _():
        wait_gather(group)


def _experts(rows, hm_packed, wg, wu, wd, bsz, cap):
    ne, d, ff = wg.shape
    tf = _pick((256, 128), ff)
    td = _pick((512, 256, 128), d)
    n_f = ff // tf
    n_d = d // td
    cp = cap + COMBINE_WIN
    grid_spec = pltpu.PrefetchScalarGridSpec(
        num_scalar_prefetch=1,
        grid=(ne, bsz, n_f + n_d),
        in_specs=[pl.BlockSpec(memory_space=pl.ANY),
                  pl.BlockSpec((1, d, tf), lambda e, b, j, *_: (e, 0, jnp.minimum(j, n_f - 1))),
                  pl.BlockSpec((1, d, tf), lambda e, b, j, *_: (e, 0, jnp.minimum(j, n_f - 1))),
                  pl.BlockSpec((1, ff, td), lambda e, b, j, *_: (e, 0, jnp.maximum(j - n_f, 0)))],
        out_specs=pl.BlockSpec((1, 1, cp, td), lambda e, b, j, *_: (e, b, 0, jnp.maximum(j - n_f, 0))),
        scratch_shapes=[pltpu.VMEM((cap, d // 2), jnp.uint32), pltpu.VMEM((cap, d), BF16),
                        pltpu.VMEM((cap, ff), BF16), pltpu.SemaphoreType.DMA(())],
    )
    return pl.pallas_call(
        functools.partial(_expert_kernel, cap=cap, n_f=n_f, n_d=n_d, tf=tf, n_groups=ne * bsz),
        out_shape=jax.ShapeDtypeStruct((ne, bsz, cp, d), BF16),
        grid_spec=grid_spec,
        compiler_params=_cparams(("arbitrary", "arbitrary", "arbitrary")),
        name="expert_ffn",
    )(rows, hm_packed, wg, wu, wd)


def _combine_kernel(start_ref, small_ref, x2_ref, pos_ref, aff_ref, g_ref, y_hbm, o_ref,
                    fast_ref, slow_ref, fast_sem, slow_sem, *, ne, nb, bsz, sub, slow_experts):
    b = pl.program_id(0)
    j = pl.program_id(1)
    steps_per_seq = nb // sub
    step = b * steps_per_seq + j
    total = bsz * steps_per_seq
    d = fast_ref.shape[3]

    def window_start(bb, e, block):
        return start_ref[(bb * ne + e) * nb + block]

    def window_copy(bb, e, block, win, dst, sem):
        rows = pl.ds(pl.multiple_of(window_start(bb, e, block), BF16_SUBLANES), win)
        return pltpu.make_async_copy(y_hbm.at[e, bb, rows, :], dst, sem)

    def small_copies(st):
        slot = st % 2
        bb = st // steps_per_seq
        block = (st - bb * steps_per_seq) * sub
        return [window_copy(bb, e, block, COMBINE_WIN_SMALL, fast_ref.at[slot, e], fast_sem.at[slot])
                for e in range(ne)]

    @pl.when((step == 0) & (small_ref[0] == 1))
    def _():
        for c in small_copies(step):
            c.start()

    nxt = jnp.minimum(step + 1, total - 1)

    @pl.when((step + 1 < total) & (small_ref[nxt] == 1))
    def _():
        for c in small_copies(nxt):
            c.start()

    def one_hot(rows, experts, block, win):
        lane = lax.broadcasted_iota(jnp.int32, (1, win), 1)
        parts = []
        for e in experts:
            rel = pos_ref[0, rows, e:e + 1] - window_start(b, e, block)
            parts.append(jnp.where(rel == lane, aff_ref[0, rows, e:e + 1], 0.0).astype(BF16))
        return jnp.concatenate(parts, axis=1)

    @pl.when(small_ref[step] == 1)
    def _():
        for c in small_copies(step):
            c.wait()
        sel = one_hot(slice(None), range(ne), j * sub, COMBINE_WIN_SMALL)
        wins = fast_ref[step % 2].reshape(ne * COMBINE_WIN_SMALL, d)
        o_ref[...] = _rms(x2_ref[...] + jnp.dot(sel, wins, preferred_element_type=F32), g_ref[...])

    @pl.when(small_ref[step] != 1)
    def _():
        for s in range(sub):
            rows = slice(s * BLOCK, (s + 1) * BLOCK)
            acc = x2_ref[rows, :]
            for first in range(0, ne, slow_experts):
                experts = range(first, first + slow_experts)
                copies = [window_copy(b, e, j * sub + s, COMBINE_WIN, slow_ref.at[e - first], slow_sem)
                          for e in experts]
                for c in copies:
                    c.start()
                for c in copies:
                    c.wait()
                sel = one_hot(rows, experts, j * sub + s, COMBINE_WIN)
                wins = slow_ref[...].reshape(slow_experts * COMBINE_WIN, d)
                acc = acc + jnp.dot(sel, wins, preferred_element_type=F32)
            o_ref[rows, :] = _rms(acc, g_ref[...])


def _combine(starts, small, x2, pos, aff, g_final, y, seq, sub):
    m, d = x2.shape
    ne, bsz = y.shape[0], y.shape[1]
    nb = seq // BLOCK
    tokens = sub * BLOCK
    steps_per_seq = nb // sub
    slow_experts = math.gcd(ne, COMBINE_SLOW_EXPERTS)
    grid_spec = pltpu.PrefetchScalarGridSpec(
        num_scalar_prefetch=2,
        grid=(bsz, steps_per_seq),
        in_specs=[pl.BlockSpec((tokens, d), lambda b, j, *_: (b * steps_per_seq + j, 0)),
                  pl.BlockSpec((1, tokens, ne), lambda b, j, *_: (b, j, 0)),
                  pl.BlockSpec((1, tokens, ne), lambda b, j, *_: (b, j, 0)),
                  pl.BlockSpec((1, d), lambda b, j, *_: (0, 0)),
                  pl.BlockSpec(memory_space=pl.ANY)],
        out_specs=pl.BlockSpec((tokens, d), lambda b, j, *_: (b * steps_per_seq + j, 0)),
        scratch_shapes=[pltpu.VMEM((2, ne, COMBINE_WIN_SMALL, d), BF16),
                        pltpu.VMEM((slow_experts, COMBINE_WIN, d), BF16),
                        pltpu.SemaphoreType.DMA((2,)), pltpu.SemaphoreType.DMA(())],
    )
    return pl.pallas_call(
        functools.partial(_combine_kernel, ne=ne, nb=nb, bsz=bsz, sub=sub, slow_experts=slow_experts),
        out_shape=jax.ShapeDtypeStruct((m, d), F32),
        grid_spec=grid_spec,
        compiler_params=_cparams(("arbitrary", "arbitrary")),
        name="moe_combine_final_norm",
    )(starts, small, x2, pos, aff, g_final.reshape(1, d).astype(F32), y)


def kernel(x, mem, g_mix, w_in, conv_w, attn_sinks, w_conv_out, w_attn_out, w_out, g_cross, g_mem, w_q_cross,
           w_kv_cross, w_o_cross, g_moe, w_router, w_gate_e, w_up_e, w_down_e, g_final):
    bsz, seq, d = x.shape
    m = bsz * seq
    cw = conv_w.shape[1]
    aw = w_attn_out.shape[0]
    n_heads = attn_sinks.shape[0]
    in_cols = w_in.shape[1]
    kvw = (in_cols - 3 * cw - aw - 2 * d) // 2
    off_b, off_c, off_u = 0, cw, 2 * cw
    off_q = 3 * cw
    off_k = off_q + aw
    off_v = off_k + kvw
    off_ga = off_v + kvw
    off_gb = off_ga + d
    ne = w_router.shape[1]
    cap = CAPACITY_FACTOR * seq // ne

    x2d = x.reshape(m, d)

    h = _rmsnorm(x2d, g_mix, BF16)
    p = _inproj(h, w_in, off_ga)
    za = _conv_mixer(p, conv_w, seq, off_b, off_c, off_u)
    slopes = jnp.power(2.0, -8.0 * (jnp.arange(n_heads, dtype=F32) + 1.0) / n_heads)
    ob = _windowed_attention(p, slopes, attn_sinks.astype(F32), bsz, seq, off_q, off_k, off_v, aw, kvw, n_heads)
    mix = _merge(za, ob, w_conv_out, w_attn_out, p, off_ga, off_gb)
    x1 = _resid_matmul(mix, w_out, x2d)

    mem_len = mem.shape[1]
    kv = _mem_kv(mem.reshape(bsz * mem_len, d), g_mem, w_kv_cross.astype(BF16)).reshape(bsz, mem_len, -1)
    wr_hi = w_router.astype(BF16)
    wr_lo = (w_router - wr_hi.astype(F32)).astype(BF16)
    x2, hm_packed, logits = _cross_attention(x1, g_cross, w_q_cross.astype(BF16), kv, w_o_cross.astype(BF16),
                                             g_moe, jnp.concatenate([wr_hi, wr_lo], axis=1), seq)

    out = _moe_and_final_norm(x2, hm_packed, logits.reshape(bsz, seq, ne), w_gate_e, w_up_e, w_down_e, g_final)
    return out.reshape(bsz, seq, d)


def _moe_and_final_norm(x2, hm_packed, logits, w_gate_e, w_up_e, w_down_e, g_final):
    bsz, seq, ne = logits.shape
    cap = CAPACITY_FACTOR * seq // ne
    pos, aff, csel, ends = _route(logits, cap)
    idx = _slot_tokens(csel, cap)
    rows = (idx + (jnp.arange(bsz, dtype=jnp.int32) * seq)[:, None, None]).transpose(1, 0, 2).reshape(-1)
    y = _experts(rows, hm_packed, w_gate_e, w_up_e, w_down_e, bsz, cap)
    first = jnp.concatenate([jnp.zeros((bsz, 1, ne), jnp.int32), ends[:, :-1, :]], axis=1)
    starts = (first // BF16_SUBLANES) * BF16_SUBLANES
    sub = math.gcd(seq // BLOCK, COMBINE_BLOCKS)
    small = jnp.all(ends[:, sub - 1::sub] - starts[:, ::sub] <= COMBINE_WIN_SMALL, axis=2).astype(jnp.int32)
    return _combine(starts.transpose(0, 2, 1).reshape(-1), small.reshape(-1), x2, pos, aff, g_final, y, seq, sub)
```
